```python
import math
import jax, jax.numpy as jnp
from jax import lax
import numpy as np

D_MODEL = 2048
BATCH = 2
SEQ = 4096
DEPTH = 4
DEC_BATCH = 8
DEC_SEQ = 4
PAST_LEN = 16384
PAGE_SIZE = 128

N_A_LAYERS = DEPTH // 2
N_B_LAYERS = DEPTH - N_A_LAYERS
RMS_EPS = 1e-6

D_INNER = 2 * D_MODEL
SSM_HEAD_DIM = 64
N_SSM_HEADS = D_INNER // SSM_HEAD_DIM
D_STATE = 128
N_SSM_GROUPS = 8
HEADS_PER_SSM_GROUP = N_SSM_HEADS // N_SSM_GROUPS
CONV_W = 4
GN = N_SSM_GROUPS * D_STATE
CONV_DIM = D_INNER + 2 * GN
D_IN_PROJ = 2 * D_INNER + 2 * GN + N_SSM_HEADS
SSD_CHUNK = 128

ATT_HEAD_DIM = 128
N_KV_HEADS = 8
WINDOWS = (128, 512, 2048)
DILATIONS = (1, 4, 16)
N_ATT_GROUPS = len(WINDOWS)
N_DIL_KEYS = WINDOWS[0] // DILATIONS[0] + 1
MAX_WINDOW = max(WINDOWS)
Q_BLOCK = 128
D_Q = N_ATT_GROUPS * N_KV_HEADS * ATT_HEAD_DIM
D_ATT_OUT = N_KV_HEADS * ATT_HEAD_DIM
NEG_INF = -1e30

N_EXPERT_GROUPS = 4
EXPERTS_PER_GROUP = 8
N_EXPERTS = N_EXPERT_GROUPS * EXPERTS_PER_GROUP
TOP_K_IN_GROUP = 2
D_EXPERT = 512

kernel_name = "yoco_ssd_dilated_window_hmoe_step"


def rms_norm(x, g):
    xf = x.astype(jnp.float32)
    xf = xf * lax.rsqrt(jnp.mean(xf * xf, axis=-1, keepdims=True) + RMS_EPS)
    return (xf * g.astype(jnp.float32)).astype(x.dtype)


def causal_depthwise_conv(xbc, conv_state, w, b):
    s = xbc.shape[1]
    xpad = jnp.concatenate([conv_state.astype(xbc.dtype), xbc], axis=1)
    out = b
    for k in range(CONV_W):
        out = out + w[k] * xpad[:, k:k + s]
    return out, xpad[:, s:]


def ssd_chunked(x, dt, a, bm, cm, init_state):
    f32 = jnp.float32
    bsz, s = x.shape[:2]
    lc = math.gcd(s, SSD_CHUNK)
    nc = s // lc
    g, hg, p, n = N_SSM_GROUPS, HEADS_PER_SSM_GROUP, SSM_HEAD_DIM, D_STATE
    xd = (x.astype(f32) * dt[..., None]).reshape(bsz, nc, lc, g, hg, p)
    adt = (dt * a).reshape(bsz, nc, lc, g, hg)
    bm = bm.astype(f32).reshape(bsz, nc, lc, g, n)
    cm = cm.astype(f32).reshape(bsz, nc, lc, g, n)
    a_cs = jnp.cumsum(adt, axis=2)
    causal = jnp.tril(jnp.ones((lc, lc), bool))[None, None, :, :, None, None]
    seg = a_cs[:, :, :, None] - a_cs[:, :, None, :]
    decay = jnp.where(causal, jnp.exp(jnp.where(causal, seg, 0.0)), 0.0)
    cb = jnp.einsum('bcign,bcjgn->bcijg', cm, bm)
    y_diag = jnp.einsum('bcijgh,bcjghp->bcighp', cb[..., None] * decay, xd)
    decay_to_end = jnp.exp(a_cs[:, :, -1:] - a_cs)
    chunk_states = jnp.einsum('bcjgn,bcjgh,bcjghp->bcghpn', bm, decay_to_end, xd)
    chunk_decay = jnp.exp(a_cs[:, :, -1])

    def carry_step(state, inp):
        st, dec = inp
        return state * dec[..., None, None] + st, state

    final, prev = lax.scan(carry_step, init_state.astype(f32).reshape(bsz, g, hg, p, n),
                           (jnp.moveaxis(chunk_states, 1, 0), jnp.moveaxis(chunk_decay, 1, 0)))
    prev = jnp.moveaxis(prev, 0, 1)
    y_off = jnp.einsum('bcign,bcghpn,bcigh->bcighp', cm, prev, jnp.exp(a_cs))
    y = (y_diag + y_off).reshape(bsz, s, g * hg, p)
    return y, final.reshape(bsz, g * hg, p, n)


def mamba2_mixer(u, conv_state, ssm_state, w_in, conv_w, conv_b, dt_bias, a_log, d_skip, gate_norm, w_out):
    f32 = jnp.float32
    bsz, s, _ = u.shape
    zxbcdt = u @ w_in
    z = zxbcdt[..., :D_INNER]
    xbc = zxbcdt[..., D_INNER:D_INNER + CONV_DIM]
    dt = zxbcdt[..., D_INNER + CONV_DIM:]
    xbc, new_conv = causal_depthwise_conv(xbc, conv_state, conv_w, conv_b)
    xbc = jax.nn.silu(xbc)
    xs = xbc[..., :D_INNER].reshape(bsz, s, N_SSM_HEADS, SSM_HEAD_DIM)
    bm = xbc[..., D_INNER:D_INNER + GN].reshape(bsz, s, N_SSM_GROUPS, D_STATE)
    cm = xbc[..., D_INNER + GN:].reshape(bsz, s, N_SSM_GROUPS, D_STATE)
    dt = jax.nn.softplus(dt.astype(f32) + dt_bias.astype(f32))
    a = -jnp.exp(a_log.astype(f32))
    y, new_ssm = ssd_chunked(xs, dt, a, bm, cm, ssm_state)
    y = y + d_skip.astype(f32)[:, None] * xs.astype(f32)
    y = y.reshape(bsz, s, D_INNER) * jax.nn.silu(z.astype(f32))
    dg = D_INNER // N_SSM_GROUPS
    y = rms_norm(y.reshape(bsz, s, N_SSM_GROUPS, dg), gate_norm.reshape(N_SSM_GROUPS, dg))
    y = y.reshape(bsz, s, D_INNER).astype(u.dtype)
    return y @ w_out, new_conv, new_ssm


def dilated_window_attention(q, k_all, v_all, past):
    f32 = jnp.float32
    bsz, s = q.shape[:2]
    qb = math.gcd(s, Q_BLOCK)
    nb = s // qb
    scale = ATT_HEAD_DIM ** -0.5
    offsets = jnp.asarray(DILATIONS, jnp.int32)[:, None] * jnp.arange(N_DIL_KEYS, dtype=jnp.int32)[None, :]
    q_blocks = jnp.moveaxis(q.reshape(bsz, nb, qb, N_ATT_GROUPS, N_KV_HEADS, ATT_HEAD_DIM), 1, 0)
    starts = past + jnp.arange(nb, dtype=jnp.int32) * qb

    def one_block(args):
        qblk, start = args
        pos = start + jnp.arange(qb, dtype=jnp.int32)
        idx = pos[None, :, None] - offsets[:, None, :]
        valid = idx >= 0
        idxc = jnp.maximum(idx, 0)
        kg = k_all[:, idxc].astype(f32)
        vg = v_all[:, idxc].astype(f32)
        sc = jnp.einsum('bqghd,bgqkhd->bghqk', qblk.astype(f32), kg) * scale
        sc = jnp.where(valid[None, :, None], sc, NEG_INF)
        m = jnp.max(sc, axis=-1, keepdims=True)
        p = jnp.exp(sc - m)
        den = jnp.sum(p, axis=-1, keepdims=True)
        o_g = jnp.einsum('bghqk,bgqkhd->bghqd', p, vg) / den
        lse = (m + jnp.log(den))[..., 0]
        w = jax.nn.softmax(lse, axis=1)
        return jnp.einsum('bghq,bghqd->bqhd', w, o_g)

    out = lax.map(one_block, (q_blocks, starts))
    return jnp.moveaxis(out, 0, 1).reshape(bsz, s, N_KV_HEADS, ATT_HEAD_DIM).astype(q.dtype)


def hier_moe(u, wg, bg, we, be, w_gate, w_up, w_down):
    f32 = jnp.float32
    shp = u.shape
    t = u.reshape(-1, D_MODEL)
    g_logits = (t @ wg + bg).astype(f32)
    g_prob = jax.nn.softmax(g_logits, axis=-1)
    g_idx = jnp.argmax(g_logits, axis=-1)
    g_w = jnp.take_along_axis(g_prob, g_idx[:, None], axis=-1)
    e_logits = (jnp.einsum('td,dge->tge', t, we) + be).astype(f32)
    e_sel = jnp.take_along_axis(e_logits, g_idx[:, None, None], axis=1)[:, 0]
    top_v, top_i = lax.top_k(e_sel, TOP_K_IN_GROUP)
    top_w = jax.nn.softmax(top_v, axis=-1)
    e_w = jnp.einsum('tk,tke->te', top_w, jax.nn.one_hot(top_i, EXPERTS_PER_GROUP, dtype=f32))
    gate = (g_w[:, :, None] * jax.nn.one_hot(g_idx, N_EXPERT_GROUPS, dtype=f32)[:, :, None]
            * e_w[:, None, :]).reshape(-1, N_EXPERTS)
    h = jax.nn.silu(jnp.einsum('td,edf->tef', t, w_gate)) * jnp.einsum('td,edf->tef', t, w_up)
    h = h * gate[..., None].astype(h.dtype)
    return jnp.einsum('tef,efd->td', h, w_down).reshape(shp)


def run_trunk(x, conv_init, ssm_init, k_past, v_past, prm):
    bsz, s, _ = x.shape
    past = k_past.shape[1]
    h = x
    conv_out, ssm_out = [], []
    k_new = v_new = k_all = v_all = None
    for layer in range(DEPTH):
        if layer < N_A_LAYERS:
            mix, cs, ss = mamba2_mixer(rms_norm(h, prm['a_norm'][layer]), conv_init[layer], ssm_init[layer],
                                       prm['a_w_in'][layer], prm['a_conv_w'][layer], prm['a_conv_b'][layer],
                                       prm['a_dt_bias'][layer], prm['a_a_log'][layer], prm['a_d_skip'][layer],
                                       prm['a_gate_norm'][layer], prm['a_w_out'][layer])
            conv_out.append(cs)
            ssm_out.append(ss)
        else:
            if layer == N_A_LAYERS:
                kv = rms_norm(h, prm['kv_norm']) @ prm['w_kv']
                k_new = kv[..., :D_ATT_OUT].reshape(bsz, s, N_KV_HEADS, ATT_HEAD_DIM)
                v_new = kv[..., D_ATT_OUT:].reshape(bsz, s, N_KV_HEADS, ATT_HEAD_DIM)
                k_all = jnp.concatenate([k_past.astype(k_new.dtype), k_new], axis=1)
                v_all = jnp.concatenate([v_past.astype(v_new.dtype), v_new], axis=1)
            b = layer - N_A_LAYERS
            q = (rms_norm(h, prm['b_norm'][b]) @ prm['b_w_q'][b]).reshape(
                bsz, s, N_ATT_GROUPS, N_KV_HEADS, ATT_HEAD_DIM)
            o = dilated_window_attention(q, k_all, v_all, past)
            mix = o.reshape(bsz, s, D_ATT_OUT).astype(h.dtype) @ prm['b_w_o'][b]
        h = h + mix
        h = h + hier_moe(rms_norm(h, prm['moe_norm'][layer]), prm['router_group_w'][layer],
                         prm['router_group_b'][layer], prm['router_expert_w'][layer],
                         prm['router_expert_b'][layer], prm['expert_w_gate'][layer],
                         prm['expert_w_up'][layer], prm['expert_w_down'][layer])
    y = rms_norm(h, prm['final_norm'])
    return y, jnp.stack(conv_out), jnp.stack(ssm_out), k_new, v_new


def setup_inputs(seed: int = 0) -> dict:
    key = jax.random.key(seed)
    ks = jax.random.split(key, 32)
    f32 = jnp.float32

    def nrm(k, shape, scale):
        return jax.random.normal(k, shape, f32) * scale

    w_buf = min(MAX_WINDOW, PAST_LEN)
    na, nbl = N_A_LAYERS, N_B_LAYERS
    dt0 = jnp.exp(jax.random.uniform(ks[10], (na, N_SSM_HEADS), f32, math.log(1e-3), math.log(1e-1)))
    return {
        'x_prompt': nrm(ks[0], (BATCH, SEQ, D_MODEL), 1.0),
        'x_sample': nrm(ks[1], (DEC_BATCH, DEC_SEQ, D_MODEL), 1.0),
        'state_conv': nrm(ks[2], (na, DEC_BATCH, CONV_W - 1, CONV_DIM), 1.0),
        'state_ssm': nrm(ks[3], (na, DEC_BATCH, N_SSM_HEADS, SSM_HEAD_DIM, D_STATE), 0.1),
        'cache_k': nrm(ks[4], (DEC_BATCH, w_buf, N_KV_HEADS, ATT_HEAD_DIM), 1.0),
        'cache_v': nrm(ks[5], (DEC_BATCH, w_buf, N_KV_HEADS, ATT_HEAD_DIM), 1.0),
        'a_norm': 1.0 + nrm(ks[6], (na, D_MODEL), 0.02),
        'a_w_in': nrm(ks[7], (na, D_MODEL, D_IN_PROJ), D_MODEL ** -0.5),
        'a_conv_w': nrm(ks[8], (na, CONV_W, CONV_DIM), CONV_W ** -0.5),
        'a_conv_b': nrm(ks[9], (na, CONV_DIM), 0.02),
        'a_dt_bias': dt0 + jnp.log(-jnp.expm1(-dt0)),
        'a_a_log': jnp.log(jax.random.uniform(ks[11], (na, N_SSM_HEADS), f32, 1.0, 16.0)),
        'a_d_skip': 1.0 + nrm(ks[12], (na, N_SSM_HEADS), 0.02),
        'a_gate_norm': 1.0 + nrm(ks[13], (na, D_INNER), 0.02),
        'a_w_out': nrm(ks[14], (na, D_INNER, D_MODEL), D_INNER ** -0.5),
        'kv_norm': 1.0 + nrm(ks[15], (D_MODEL,), 0.02),
        'w_kv': nrm(ks[16], (D_MODEL, 2 * D_ATT_OUT), D_MODEL ** -0.5),
        'b_norm': 1.0 + nrm(ks[17], (nbl, D_MODEL), 0.02),
        'b_w_q': nrm(ks[18], (nbl, D_MODEL, D_Q), D_MODEL ** -0.5),
        'b_w_o': nrm(ks[19], (nbl, D_ATT_OUT, D_MODEL), D_ATT_OUT ** -0.5),
        'moe_norm': 1.0 + nrm(ks[20], (DEPTH, D_MODEL), 0.02),
        'router_group_w': nrm(ks[21], (DEPTH, D_MODEL, N_EXPERT_GROUPS), D_MODEL ** -0.5),
        'router_group_b': nrm(ks[22], (DEPTH, N_EXPERT_GROUPS), 0.01),
        'router_expert_w': nrm(ks[23], (DEPTH, D_MODEL, N_EXPERT_GROUPS, EXPERTS_PER_GROUP), D_MODEL ** -0.5),
        'router_expert_b': nrm(ks[24], (DEPTH, N_EXPERT_GROUPS, EXPERTS_PER_GROUP), 0.01),
        'expert_w_gate': nrm(ks[25], (DEPTH, N_EXPERTS, D_MODEL, D_EXPERT), D_MODEL ** -0.5),
        'expert_w_up': nrm(ks[26], (DEPTH, N_EXPERTS, D_MODEL, D_EXPERT), D_MODEL ** -0.5),
        'expert_w_down': nrm(ks[27], (DEPTH, N_EXPERTS, D_EXPERT, D_MODEL), D_EXPERT ** -0.5),
        'final_norm': 1.0 + nrm(ks[28], (D_MODEL,), 0.02),
    }


def reference(x_prompt, x_sample, state_conv, state_ssm, cache_k, cache_v,
              a_norm, a_w_in, a_conv_w, a_conv_b, a_dt_bias, a_a_log, a_d_skip, a_gate_norm, a_w_out,
              kv_norm, w_kv, b_norm, b_w_q, b_w_o,
              moe_norm, router_group_w, router_group_b, router_expert_w, router_expert_b,
              expert_w_gate, expert_w_up, expert_w_down, final_norm):
    prm = {
        'a_norm': a_norm, 'a_w_in': a_w_in, 'a_conv_w': a_conv_w, 'a_conv_b': a_conv_b,
        'a_dt_bias': a_dt_bias, 'a_a_log': a_a_log, 'a_d_skip': a_d_skip, 'a_gate_norm': a_gate_norm,
        'a_w_out': a_w_out, 'kv_norm': kv_norm, 'w_kv': w_kv, 'b_norm': b_norm, 'b_w_q': b_w_q,
        'b_w_o': b_w_o, 'moe_norm': moe_norm, 'router_group_w': router_group_w,
        'router_group_b': router_group_b, 'router_expert_w': router_expert_w,
        'router_expert_b': router_expert_b, 'expert_w_gate': expert_w_gate, 'expert_w_up': expert_w_up,
        'expert_w_down': expert_w_down, 'final_norm': final_norm,
    }
    bsz, s, _ = x_prompt.shape
    conv0 = jnp.zeros((N_A_LAYERS, bsz, CONV_W - 1, CONV_DIM), x_prompt.dtype)
    ssm0 = jnp.zeros((N_A_LAYERS, bsz, N_SSM_HEADS, SSM_HEAD_DIM, D_STATE), jnp.float32)
    k0 = jnp.zeros((bsz, 0, N_KV_HEADS, ATT_HEAD_DIM), x_prompt.dtype)
    y_prompt, conv_p, ssm_p, k_p, v_p = run_trunk(x_prompt, conv0, ssm0, k0, k0, prm)
    keep = min(MAX_WINDOW, s)
    k_prompt = k_p[:, s - keep:]
    v_prompt = v_p[:, s - keep:]
    y_sample, conv_s, ssm_s, k_sample, v_sample = run_trunk(x_sample, state_conv, state_ssm, cache_k, cache_v, prm)
    return (y_prompt, y_sample, conv_p, ssm_p, k_prompt, v_prompt, conv_s, ssm_s, k_sample, v_sample)
```

```python
import functools
import math

import jax
import jax.numpy as jnp
from jax import lax
from jax.experimental import pallas as pl
from jax.experimental.pallas import tpu as pltpu

F32 = jnp.float32
BF16 = jnp.bfloat16

V7X_VMEM_BYTES = 64 * 1024 * 1024
V7X_LANES = 128
V7X_SUBLANES = 8
VMEM_LIMIT = 56 * 1024 * 1024

RMS_EPS = 1e-6
SSM_HEAD_DIM = 64
N_SSM_GROUPS = 8
D_STATE = 128
CONV_W = 4
SSD_CHUNK = 128
ATT_HEAD_DIM = 128
N_KV_HEADS = 8
DILATIONS = (1, 4, 16)
N_DIL_KEYS = 129
assert all(d & (d - 1) == 0 for d in DILATIONS)
N_EXPERT_GROUPS = 4
EXPERTS_PER_GROUP = 8
N_EXPERTS = N_EXPERT_GROUPS * EXPERTS_PER_GROUP
NEG_INF = -1e30

TOKEN_TILE = 256
MM_ROW_TILE = 768
MOE_ROW_TILE = 256


def _cparams(sem):
    return pltpu.CompilerParams(dimension_semantics=sem, vmem_limit_bytes=VMEM_LIMIT)


def _norm_kernel(*refs, n_add, n_out, emit_sum):
    h = refs[0][...]
    for a in refs[1:1 + n_add]:
        h = h + a[...]
    g_ref = refs[1 + n_add]
    outs = refs[2 + n_add:]
    if emit_sum:
        outs[0][...] = h
        outs = outs[1:]
    xn = h * lax.rsqrt(jnp.mean(h * h, axis=-1, keepdims=True) + RMS_EPS)
    for k in range(n_out):
        outs[k][...] = (xn * g_ref[k:k + 1, :]).astype(outs[k].dtype)


def add_norm(h, adds, gains, out_dtype, emit_sum):
    tp, d = h.shape
    n_out = gains.shape[0]
    tm = TOKEN_TILE
    in_specs = [pl.BlockSpec((tm, d), lambda i: (i, 0))]
    args = [h]
    for arr, off in adds:
        in_specs.append(pl.BlockSpec((tm, d), functools.partial(lambda i, o: (i + o, 0), o=off)))
        args.append(arr)
    in_specs.append(pl.BlockSpec((n_out, d), lambda i: (0, 0)))
    args.append(gains)
    out_shape, out_specs = [], []
    if emit_sum:
        out_shape.append(jax.ShapeDtypeStruct((tp, d), F32))
        out_specs.append(pl.BlockSpec((tm, d), lambda i: (i, 0)))
    for _ in range(n_out):
        out_shape.append(jax.ShapeDtypeStruct((tp, d), out_dtype))
        out_specs.append(pl.BlockSpec((tm, d), lambda i: (i, 0)))
    return pl.pallas_call(
        functools.partial(_norm_kernel, n_add=len(adds), n_out=n_out, emit_sum=emit_sum),
        grid=(tp // tm,), in_specs=in_specs, out_specs=out_specs, out_shape=out_shape,
        compiler_params=_cparams(("parallel",)), name="add_norm",
    )(*args)


def _mm_kernel(*refs, has_res):
    if has_res:
        x_ref, w_ref, r_ref, o_ref, wb_ref = refs
    else:
        x_ref, w_ref, o_ref, wb_ref = refs

    @pl.when(pl.program_id(1) == 0)
    def _():
        wb_ref[...] = w_ref[...].astype(BF16)

    acc = jnp.dot(x_ref[...].astype(BF16), wb_ref[...], preferred_element_type=F32)
    if has_res:
        acc = r_ref[...] + acc
    o_ref[...] = acc


def matmul(x, w, layer, n_cols, tn, res=None, tm=MM_ROW_TILE):
    tp, k = x.shape
    assert tp % tm == 0 and n_cols % tn == 0
    if layer is None:
        w_spec = pl.BlockSpec((k, tn), lambda j, i: (0, j))
    else:
        w_spec = pl.BlockSpec((None, k, tn), lambda j, i: (layer, 0, j))
    in_specs = [pl.BlockSpec((tm, k), lambda j, i: (i, 0)), w_spec]
    args = [x, w]
    if res is not None:
        in_specs.append(pl.BlockSpec((tm, tn), lambda j, i: (i, j)))
        args.append(res)
    return pl.pallas_call(
        functools.partial(_mm_kernel, has_res=res is not None),
        grid=(n_cols // tn, tp // tm), in_specs=in_specs,
        out_specs=pl.BlockSpec((tm, tn), lambda j, i: (i, j)),
        out_shape=jax.ShapeDtypeStruct((tp, n_cols), F32),
        scratch_shapes=[pltpu.VMEM((k, tn), BF16)],
        compiler_params=_cparams(("arbitrary", "arbitrary")), name="proj",
    )(*args)


def _silu(x):
    return x * (1.0 / (1.0 + jnp.exp(-x)))


def _split2(v):
    hi = v.astype(BF16)
    lo = (v - hi.astype(F32)).astype(BF16)
    return hi, lo


def _ssd_kernel(z_ref, x_ref, b_ref, c_ref, dt_ref, conv0_ref, ssm0_ref,
                cw_ref, cb_ref, dtb_ref, alog_ref, dsk_ref, gn_ref, e_ref,
                y_ref, convo_ref, ssmo_ref,
                xpad, xs_s, xlo_s, xhi_s, xw_s, bm_s, bmt_s, cm_s, y_s, ea_s, state_s,
                *, n_valid, has_init):
    L = SSD_CHUNK
    d_inner = xs_s.shape[1]
    gn = bm_s.shape[1]
    n_heads = d_inner // SSM_HEAD_DIM
    pair = 2 * SSM_HEAD_DIM
    gcols = d_inner // N_SSM_GROUPS
    c = pl.program_id(1)
    pad0 = V7X_SUBLANES
    keep = CONV_W - 1

    @pl.when(c == 0)
    def _():
        if has_init:
            xpad[pad0 - keep:pad0, :] = conv0_ref[...]
            for g in range(N_SSM_GROUPS):
                state_s[:, g * gcols:(g + 1) * gcols] = ssm0_ref[g * gcols:(g + 1) * gcols, :].T
        else:
            xpad[0:pad0, :] = jnp.zeros((pad0, xpad.shape[1]), F32)
            state_s[...] = jnp.zeros(state_s.shape, F32)

    xpad[pad0:pad0 + L, 0:d_inner] = x_ref[...]
    xpad[pad0:pad0 + L, d_inner:d_inner + gn] = b_ref[...]
    xpad[pad0:pad0 + L, d_inner + gn:d_inner + 2 * gn] = c_ref[...]
    lane = lax.broadcasted_iota(jnp.int32, (L, 512), 1)
    even_head = (lane // SSM_HEAD_DIM) % 2 == 0
    strip = 512
    for s in range(xpad.shape[1] // strip):
        cs = slice(s * strip, (s + 1) * strip)
        acc = jnp.broadcast_to(cb_ref[:, cs], (L, strip))
        for k in range(CONV_W):
            acc = acc + cw_ref[k:k + 1, cs] * xpad[pad0 - keep + k:pad0 - keep + k + L, cs]
        v = _silu(acc)
        if s * strip < d_inner:
            xs_s[:, cs] = v
            vb = v.astype(BF16)
            xlo_s[:, cs] = jnp.where(even_head, vb, jnp.zeros_like(vb))
            xhi_s[:, cs] = jnp.where(even_head, jnp.zeros_like(vb), vb)
        elif s * strip < d_inner + gn:
            o = s * strip - d_inner
            bm_s[:, o:o + strip] = v.astype(BF16)
            for q in range(strip // D_STATE):
                gi = o // D_STATE + q
                bmt_s[:, gi * L:(gi + 1) * L] = v[:, q * D_STATE:(q + 1) * D_STATE].T.astype(BF16)
        else:
            o = s * strip - d_inner - gn
            cm_s[:, o:o + strip] = v.astype(BF16)

    tail = xpad[pad0 + n_valid - keep:pad0 + n_valid, :]
    xpad[pad0 - keep:pad0, :] = tail

    @pl.when(c == pl.num_programs(1) - 1)
    def _():
        convo_ref[...] = tail

    row = lax.broadcasted_iota(jnp.int32, (L, V7X_LANES), 0)
    dtv = dt_ref[...] + dtb_ref[...]
    dt = jnp.maximum(dtv, 0.0) + jnp.log(1.0 + jnp.exp(-jnp.abs(dtv)))
    dt = jnp.where(row < n_valid, dt, 0.0)
    a = -jnp.exp(alog_ref[...])
    adt = dt * a
    ii = lax.broadcasted_iota(jnp.int32, (L, L), 0)
    jj = lax.broadcasted_iota(jnp.int32, (L, L), 1)
    causal = ii >= jj
    tri = jnp.where(causal, 1.0, 0.0).astype(F32)
    a_cs = jnp.dot(tri, adt, preferred_element_type=F32, precision=lax.Precision.HIGHEST)
    a_cs_t = a_cs.T
    dt_t = dt.T
    ea = jnp.exp(a_cs)
    dte = dt * jnp.exp(a_cs[L - 1:L, :] - a_cs)
    both = jnp.concatenate([ea, dte], axis=0)
    hi, lo = _split2(both)
    for s in range(d_inner // 512):
        cs = slice(s * 512, (s + 1) * 512)
        ex = (jnp.dot(hi, e_ref[:, cs], preferred_element_type=F32)
              + jnp.dot(lo, e_ref[:, cs], preferred_element_type=F32))
        ea_s[:, cs] = ex[0:L]
        xw_s[:, cs] = (xs_s[:, cs] * ex[L:2 * L]).astype(BF16)

    for g in range(N_SSM_GROUPS):
        gs = slice(g * gcols, (g + 1) * gcols)
        ns = slice(g * D_STATE, (g + 1) * D_STATE)
        cg = cm_s[:, ns]
        cb = lax.dot_general(cg, bm_s[:, ns], (((1,), (1,)), ((), ())), preferred_element_type=F32)
        st = state_s[:, gs]
        y_off = jnp.dot(cg, st.astype(BF16), preferred_element_type=F32)
        for q in range(gcols // pair):
            col = slice(g * gcols + q * pair, g * gcols + (q + 1) * pair)
            acc = None
            for hh, xsrc in ((0, xlo_s), (1, xhi_s)):
                h = (g * gcols + q * pair) // SSM_HEAD_DIM + hh
                seg = a_cs[:, h:h + 1] - a_cs_t[h:h + 1, :]
                dec = jnp.exp(jnp.where(causal, seg, NEG_INF))
                m = (cb * dec * dt_t[h:h + 1, :]).astype(BF16)
                part = jnp.dot(m, xsrc[:, col], preferred_element_type=F32)
                acc = part if acc is None else acc + part
            lc = slice(q * pair, (q + 1) * pair)
            y_s[:, col] = acc + y_off[:, lc] * ea_s[:, col] + dsk_ref[:, col] * xs_s[:, col]
        new = st * ea_s[L - 1:L, gs] + jnp.dot(bmt_s[:, g * L:(g + 1) * L], xw_s[:, gs],
                                               preferred_element_type=F32)
        state_s[:, gs] = new

    for g in range(N_SSM_GROUPS):
        gs = slice(g * gcols, (g + 1) * gcols)
        v = y_s[:, gs] * _silu(z_ref[:, gs])
        v = v * lax.rsqrt(jnp.mean(v * v, axis=-1, keepdims=True) + RMS_EPS)
        y_ref[:, gs] = (v * gn_ref[:, gs]).astype(y_ref.dtype)

    @pl.when(c == pl.num_programs(1) - 1)
    def _():
        for g in range(N_SSM_GROUPS):
            ssmo_ref[g * gcols:(g + 1) * gcols, :] = state_s[:, g * gcols:(g + 1) * gcols].T


def ssd_mixer(zxbc, dt_raw, row0, n_seq, n_chunks, n_valid, conv0, ssm0, prm, layer, d_inner, gn):
    L = SSD_CHUNK
    assert row0 % L == 0
    rb = row0 // L
    conv_dim = d_inner + 2 * gn
    has_init = conv0 is not None
    if not has_init:
        conv0 = jnp.zeros((n_seq, CONV_W - 1, conv_dim), F32)
        ssm0 = jnp.zeros((n_seq, 8, D_STATE), F32)
    n_heads = d_inner // SSM_HEAD_DIM
    rows = n_seq * n_chunks * L

    def rmap(b, c):
        return rb + b * n_chunks + c

    zb = d_inner // gn
    in_specs = [
        pl.BlockSpec((L, d_inner), lambda b, c: (rmap(b, c), 0)),
        pl.BlockSpec((L, d_inner), lambda b, c: (rmap(b, c), 1)),
        pl.BlockSpec((L, gn), lambda b, c: (rmap(b, c), 2 * zb)),
        pl.BlockSpec((L, gn), lambda b, c: (rmap(b, c), 2 * zb + 1)),
        pl.BlockSpec((L, V7X_LANES), lambda b, c: (rmap(b, c), 0)),
        pl.BlockSpec((None, CONV_W - 1, conv_dim), lambda b, c: (b, 0, 0)),
        pl.BlockSpec((None,) + ssm0.shape[1:], lambda b, c: (b, 0, 0)),
        pl.BlockSpec((None, CONV_W, conv_dim), lambda b, c: (layer, 0, 0)),
        pl.BlockSpec((None, 1, conv_dim), lambda b, c: (layer, 0, 0)),
        pl.BlockSpec((None, 1, V7X_LANES), lambda b, c: (layer, 0, 0)),
        pl.BlockSpec((None, 1, V7X_LANES), lambda b, c: (layer, 0, 0)),
        pl.BlockSpec((None, 1, d_inner), lambda b, c: (layer, 0, 0)),
        pl.BlockSpec((None, 1, d_inner), lambda b, c: (layer, 0, 0)),
        pl.BlockSpec((V7X_LANES, d_inner), lambda b, c: (0, 0)),
    ]
    out_shape = [
        jax.ShapeDtypeStruct((rows, d_inner), BF16),
        jax.ShapeDtypeStruct((n_seq, CONV_W - 1, conv_dim), F32),
        jax.ShapeDtypeStruct((n_seq, n_heads * SSM_HEAD_DIM, D_STATE), F32),
    ]
    out_specs = [
        pl.BlockSpec((L, d_inner), lambda b, c: (b * n_chunks + c, 0)),
        pl.BlockSpec((None, CONV_W - 1, conv_dim), lambda b, c: (b, 0, 0)),
        pl.BlockSpec((None, n_heads * SSM_HEAD_DIM, D_STATE), lambda b, c: (b, 0, 0)),
    ]
    scratch = [
        pltpu.VMEM((L + V7X_SUBLANES, conv_dim), F32),
        pltpu.VMEM((L, d_inner), F32),
        pltpu.VMEM((L, d_inner), BF16),
        pltpu.VMEM((L, d_inner), BF16),
        pltpu.VMEM((L, d_inner), BF16),
        pltpu.VMEM((L, gn), BF16),
        pltpu.VMEM((D_STATE, N_SSM_GROUPS * L), BF16),
        pltpu.VMEM((L, gn), BF16),
        pltpu.VMEM((L, d_inner), F32),
        pltpu.VMEM((L, d_inner), F32),
        pltpu.VMEM((D_STATE, d_inner), F32),
    ]
    return pl.pallas_call(
        functools.partial(_ssd_kernel, n_valid=n_valid, has_init=has_init),
        grid=(n_seq, n_chunks), in_specs=in_specs, out_specs=out_specs, out_shape=out_shape,
        scratch_shapes=scratch,
        compiler_params=_cparams(("arbitrary", "arbitrary")), name="ssd_mixer",
    )(zxbc, zxbc, zxbc, zxbc, dt_raw, conv0, ssm0,
      prm["conv_w"], prm["conv_b"], prm["dt_bias"], prm["a_log"], prm["d_skip"], prm["gate_norm"],
      prm["expand"])


def _attn_prompt_kernel(q0_ref, q1_ref, q2_ref, k_ref, v_ref, o_ref, acc_s, m_s, l_s, *, seq):
    QB = 128
    scale = ATT_HEAD_DIM ** -0.5
    qq = lax.broadcasted_iota(jnp.int32, (QB, QB), 0)
    kk = lax.broadcasted_iota(jnp.int32, (QB, QB), 1)
    cur_ok = kk <= qq
    kq = kk - qq

    def scores(q, k_rows, ok):
        s = lax.dot_general(q, k_rows.astype(BF16), (((1,), (1,)), ((), ())),
                            preferred_element_type=F32) * scale
        return jnp.where(ok, s, NEG_INF)

    for g, (q_ref, d) in enumerate(zip((q0_ref, q1_ref, q2_ref), DILATIONS)):
        nb_r = seq // d // QB
        first, last = g == 0, g == len(DILATIONS) - 1

        def body(idx, carry, q_ref=q_ref, d=d, nb_r=nb_r, first=first, last=last):
            r = idx // nb_r
            mb = idx % nb_r
            start = r + d * QB * mb
            pstart = jnp.maximum(start - d * QB, r)
            prev_min = jnp.where(mb > 0, 0, QB)
            rows = pl.ds(start, QB, stride=d) if d > 1 else pl.ds(start, QB)
            prows = pl.ds(pstart, QB, stride=d) if d > 1 else pl.ds(pstart, QB)
            q = q_ref[rows, :].astype(BF16)
            s_c = scores(q, k_ref[rows, :], cur_ok)
            s_p = scores(q, k_ref[prows, :], kq >= prev_min)
            m_blk = jnp.maximum(jnp.max(s_c, axis=-1, keepdims=True), jnp.max(s_p, axis=-1, keepdims=True))
            if first:
                m_new = jnp.broadcast_to(m_blk, (QB, QB))
            else:
                m_old = m_s[rows, :]
                m_new = jnp.maximum(m_old, m_blk)
            p_c = jnp.exp(s_c - m_new)
            p_p = jnp.exp(s_p - m_new)
            l_blk = jnp.sum(p_c, axis=-1, keepdims=True) + jnp.sum(p_p, axis=-1, keepdims=True)
            pv = (jnp.dot(p_c.astype(BF16), v_ref[rows, :].astype(BF16), preferred_element_type=F32)
                  + jnp.dot(p_p.astype(BF16), v_ref[prows, :].astype(BF16), preferred_element_type=F32))
            if first:
                l_new = jnp.broadcast_to(l_blk, (QB, QB))
                acc = pv
            else:
                alpha = jnp.exp(m_old - m_new)
                l_new = alpha * l_s[rows, :] + l_blk
                acc = alpha * acc_s[rows, :] + pv
            if last:
                o_ref[rows, :] = acc / l_new
            else:
                m_s[rows, :] = m_new
                l_s[rows, :] = l_new
                acc_s[rows, :] = acc
            return carry

        lax.fori_loop(0, seq // QB, body, 0)


def attn_prompt(q, kv, n_seq, seq):
    hd = ATT_HEAD_DIM
    nh = N_KV_HEADS

    def qspec(g):
        return pl.BlockSpec((seq, hd), lambda b, h: (b, g * nh + h))

    return pl.pallas_call(
        functools.partial(_attn_prompt_kernel, seq=seq),
        grid=(n_seq, nh),
        in_specs=[qspec(0), qspec(1), qspec(2),
                  pl.BlockSpec((seq, hd), lambda b, h: (b, h)),
                  pl.BlockSpec((seq, hd), lambda b, h: (b, nh + h))],
        out_specs=pl.BlockSpec((seq, hd), lambda b, h: (b, h)),
        out_shape=jax.ShapeDtypeStruct((n_seq * seq, nh * hd), F32),
        scratch_shapes=[pltpu.VMEM((seq, hd), F32)] * 3,
        compiler_params=_cparams(("parallel", "parallel")), name="attn_prompt",
    )(q, q, q, kv, kv)


def _attn_sample_kernel(q_ref, kn_ref, vn_ref, kc_ref, vc_ref, o_ref, *, n_new, past):
    QR = q_ref.shape[0]
    scale = ATT_HEAD_DIM ** -0.5
    q = q_ref[...].astype(BF16)

    def dist_ok(dist, dil):
        return jnp.logical_and(jnp.logical_and(dist >= 0, dist <= dil * (N_DIL_KEYS - 1)),
                               jnp.bitwise_and(dist, dil - 1) == 0)

    def mask(n_keys, key0):
        rr = lax.broadcasted_iota(jnp.int32, (QR, n_keys), 0)
        kpos = lax.broadcasted_iota(jnp.int32, (QR, n_keys), 1) + key0
        g = rr // n_new
        qpos = past + rr % n_new
        dil = jnp.where(g == 0, DILATIONS[0], jnp.where(g == 1, DILATIONS[1], DILATIONS[2]))
        ok = dist_ok(qpos - kpos, dil)
        return jnp.logical_and(ok, g < len(DILATIONS))

    ok_c = mask(past, 0)
    ok_n = jnp.logical_and(mask(kn_ref.shape[0], past),
                           lax.broadcasted_iota(jnp.int32, (QR, kn_ref.shape[0]), 1) < n_new)
    s_c = lax.dot_general(q, kc_ref[...].astype(BF16), (((1,), (1,)), ((), ())),
                          preferred_element_type=F32) * scale
    s_n = lax.dot_general(q, kn_ref[...].astype(BF16), (((1,), (1,)), ((), ())),
                          preferred_element_type=F32) * scale
    s_c = jnp.where(ok_c, s_c, NEG_INF)
    s_n = jnp.where(ok_n, s_n, NEG_INF)
    m = jnp.maximum(jnp.max(s_c, axis=-1, keepdims=True), jnp.max(s_n, axis=-1, keepdims=True))
    p_c = jnp.exp(s_c - m)
    p_n = jnp.exp(s_n - m)
    l = jnp.sum(p_c, axis=-1, keepdims=True) + jnp.sum(p_n, axis=-1, keepdims=True)
    pv = (jnp.dot(p_c.astype(BF16), vc_ref[...].astype(BF16), preferred_element_type=F32)
          + jnp.dot(p_n.astype(BF16), vn_ref[...].astype(BF16), preferred_element_type=F32))
    ms = [m[g * n_new:(g + 1) * n_new] for g in range(len(DILATIONS))]
    m_all = functools.reduce(jnp.maximum, ms)
    num = jnp.zeros((n_new, ATT_HEAD_DIM), F32)
    den = jnp.zeros((n_new, 1), F32)
    for g in range(len(DILATIONS)):
        w = jnp.exp(ms[g] - m_all)
        num = num + w * pv[g * n_new:(g + 1) * n_new]
        den = den + w * l[g * n_new:(g + 1) * n_new]
    o_ref[...] = num / den


def attn_sample(q_s, k_new, v_new, cache_k, cache_v, n_new):
    nb, nh, qr, hd = q_s.shape
    past = cache_k.shape[1]
    return pl.pallas_call(
        functools.partial(_attn_sample_kernel, n_new=n_new, past=past),
        grid=(nb, nh),
        in_specs=[pl.BlockSpec((None, None, qr, hd), lambda b, h: (b, h, 0, 0)),
                  pl.BlockSpec((None, None, k_new.shape[2], hd), lambda b, h: (b, h, 0, 0)),
                  pl.BlockSpec((None, None, k_new.shape[2], hd), lambda b, h: (b, h, 0, 0)),
                  pl.BlockSpec((None, past, hd), lambda b, h: (b, 0, h)),
                  pl.BlockSpec((None, past, hd), lambda b, h: (b, 0, h))],
        out_specs=pl.BlockSpec((None, None, n_new, hd), lambda b, h: (b, h, 0, 0)),
        out_shape=jax.ShapeDtypeStruct((nb, nh, n_new, hd), F32),
        compiler_params=_cparams(("parallel", "parallel")), name="attn_sample",
    )(q_s, k_new, v_new, cache_k, cache_v)


def _router_kernel(h_ref, g_ref, w_ref, b_ref, xn_ref, idx_ref, wt_ref):
    h = h_ref[...]
    xn = h * lax.rsqrt(jnp.mean(h * h, axis=-1, keepdims=True) + RMS_EPS) * g_ref[...]
    xn_ref[...] = xn
    logits = jnp.dot(xn, w_ref[...], preferred_element_type=F32,
                     precision=lax.Precision.HIGHEST) + b_ref[...]
    tm = logits.shape[0]
    lane = lax.broadcasted_iota(jnp.int32, logits.shape, 1)
    big = jnp.int32(V7X_LANES)
    is_g = lane < N_EXPERT_GROUPS
    gl = jnp.where(is_g, logits, NEG_INF)
    gmax = jnp.max(gl, axis=-1, keepdims=True)
    g_idx = jnp.min(jnp.where(jnp.logical_and(is_g, gl == gmax), lane, big), axis=-1, keepdims=True)
    g_w = 1.0 / jnp.sum(jnp.where(is_g, jnp.exp(gl - gmax), 0.0), axis=-1, keepdims=True)
    lo = N_EXPERT_GROUPS + g_idx * EXPERTS_PER_GROUP
    in_grp = jnp.logical_and(lane >= lo, lane < lo + EXPERTS_PER_GROUP)
    el = jnp.where(in_grp, logits, NEG_INF)
    v1 = jnp.max(el, axis=-1, keepdims=True)
    i1 = jnp.min(jnp.where(jnp.logical_and(in_grp, el == v1), lane, big), axis=-1, keepdims=True)
    el2 = jnp.where(lane == i1, NEG_INF, el)
    v2 = jnp.max(el2, axis=-1, keepdims=True)
    i2 = jnp.min(jnp.where(jnp.logical_and(in_grp, jnp.logical_and(el2 == v2, lane != i1)), lane, big),
                 axis=-1, keepdims=True)
    e2 = jnp.exp(v2 - v1)
    w1 = g_w / (1.0 + e2)
    w2 = g_w * e2 / (1.0 + e2)
    col = lax.broadcasted_iota(jnp.int32, (tm, 2), 1)
    idx_ref[...] = jnp.where(col == 0, i1, i2) - N_EXPERT_GROUPS
    wt_ref[...] = jnp.where(col == 0, w1, w2)


def router(h, gain, w_cat, b_cat):
    tp, d = h.shape
    tm = TOKEN_TILE
    return pl.pallas_call(
        _router_kernel, grid=(tp // tm,),
        in_specs=[pl.BlockSpec((tm, d), lambda i: (i, 0)),
                  pl.BlockSpec((1, d), lambda i: (0, 0)),
                  pl.BlockSpec((d, V7X_LANES), lambda i: (0, 0)),
                  pl.BlockSpec((1, V7X_LANES), lambda i: (0, 0))],
        out_specs=[pl.BlockSpec((tm, d), lambda i: (i, 0)),
                   pl.BlockSpec((tm, 2), lambda i: (i, 0)),
                   pl.BlockSpec((tm, 2), lambda i: (i, 0))],
        out_shape=[jax.ShapeDtypeStruct((tp, d), F32),
                   jax.ShapeDtypeStruct((tp, 2), jnp.int32),
                   jax.ShapeDtypeStruct((tp, 2), F32)],
        compiler_params=_cparams(("parallel",)), name="router",
    )(h, gain, w_cat, b_cat)


def _expert_kernel(te_ref, nv_ref, src_ref, dst_ref,
                   xn_hbm, rw_ref, wg_ref, wu_ref, wd_ref,
                   y_hbm,
                   xbuf, ybuf, wgb, wub, wdb, gsem, ssem):
    i = pl.program_id(0)
    n_tiles = pl.num_programs(0)
    tm = xbuf.shape[1]

    def n_valid(tile):
        ok = jnp.logical_and(tile >= 0, tile < n_tiles)
        return jnp.where(ok, nv_ref[jnp.clip(tile, 0, n_tiles - 1)], 0)

    def gather_copy(tile, slot, r):
        return pltpu.make_async_copy(xn_hbm.at[pl.ds(src_ref[tile * tm + r], 1)],
                                     xbuf.at[slot, pl.ds(r, 1)], gsem.at[slot])

    def scatter_copy(tile, slot, r):
        return pltpu.make_async_copy(ybuf.at[slot, pl.ds(r, 1)],
                                     y_hbm.at[pl.ds(dst_ref[tile * tm + r], 1)], ssem.at[slot])

    def start_gather(tile, slot):
        lax.fori_loop(0, tm, lambda r, c: (gather_copy(tile, slot, r).start(), c)[1], 0)

    def wait_gather(tile, slot):
        lax.fori_loop(0, tm, lambda r, c: (gather_copy(tile, slot, r).wait(), c)[1], 0)

    def start_scatter(tile, slot):
        lax.fori_loop(0, n_valid(tile), lambda r, c: (scatter_copy(tile, slot, r).start(), c)[1], 0)

    def wait_scatter(tile, slot):
        lax.fori_loop(0, n_valid(tile), lambda r, c: (scatter_copy(tile, slot, r).wait(), c)[1], 0)

    slot = i % 2
    used = n_valid(i) > 0

    @pl.when(jnp.logical_and(i == 0, used))
    def _():
        start_gather(0, 0)

    @pl.when(n_valid(i + 1) > 0)
    def _():
        start_gather(i + 1, 1 - slot)

    wait_scatter(i - 2, slot)

    @pl.when(used)
    def _():
        prev = te_ref[jnp.maximum(i - 1, 0)]

        @pl.when(jnp.logical_or(i == 0, te_ref[i] != prev))
        def _():
            wgb[...] = wg_ref[...].astype(BF16)
            wub[...] = wu_ref[...].astype(BF16)
            wdb[...] = wd_ref[...].astype(BF16)

        wait_gather(i, slot)
        x = xbuf[slot].astype(BF16)
        hg = jnp.dot(x, wgb[...], preferred_element_type=F32)
        hu = jnp.dot(x, wub[...], preferred_element_type=F32)
        act = (_silu(hg) * hu * rw_ref[...]).astype(BF16)
        ybuf[slot] = jnp.dot(act, wdb[...], preferred_element_type=F32)
        start_scatter(i, slot)

    @pl.when(i == n_tiles - 1)
    def _():
        wait_scatter(i - 1, 1 - slot)
        wait_scatter(i, slot)


def expert_ffn(xn, tile_expert, tile_nvalid, row_src, row_dst, row_w, w_gate, w_up, w_down, layer):
    tp, d = xn.shape
    tm = MOE_ROW_TILE
    n_tiles = row_src.shape[0] // tm
    de = w_gate.shape[-1]

    def wmap(i, te, nv, src, dst):
        return (layer, te[i], 0, 0)

    grid_spec = pltpu.PrefetchScalarGridSpec(
        num_scalar_prefetch=4, grid=(n_tiles,),
        in_specs=[pl.BlockSpec(memory_space=pl.ANY),
                  pl.BlockSpec((tm, 1), lambda i, te, nv, src, dst: (i, 0)),
                  pl.BlockSpec((None, None, d, de), wmap),
                  pl.BlockSpec((None, None, d, de), wmap),
                  pl.BlockSpec((None, None, de, d), wmap)],
        out_specs=pl.BlockSpec(memory_space=pl.ANY),
        scratch_shapes=[pltpu.VMEM((2, tm, d), F32), pltpu.VMEM((2, tm, d), F32),
                        pltpu.VMEM((d, de), BF16), pltpu.VMEM((d, de), BF16), pltpu.VMEM((de, d), BF16),
                        pltpu.SemaphoreType.DMA((2,)), pltpu.SemaphoreType.DMA((2,))],
    )
    return pl.pallas_call(
        _expert_kernel, grid_spec=grid_spec,
        out_shape=jax.ShapeDtypeStruct((2 * tp, d), F32),
        compiler_params=_cparams(("arbitrary",)), name="expert_ffn",
    )(tile_expert, tile_nvalid, row_src, row_dst, xn, row_w, w_gate, w_up, w_down)


def moe_plan(idx, wts, tp):
    tm = MOE_ROW_TILE
    n_assign = 2 * tp
    r_max = n_assign + N_EXPERTS * (tm - 1)
    r_max = -(-r_max // tm) * tm
    n_tiles = r_max // tm
    e_flat = idx.reshape(-1)
    order = jnp.argsort(e_flat, stable=True).astype(jnp.int32)
    counts = jnp.sum(e_flat[:, None] == jnp.arange(N_EXPERTS, dtype=jnp.int32)[None, :], axis=0,
                     dtype=jnp.int32)
    padded = (counts + tm - 1) // tm * tm
    pend = jnp.cumsum(padded)
    pstart = pend - padded
    start = jnp.cumsum(counts) - counts
    rho = jnp.arange(r_max, dtype=jnp.int32)
    e_of = jnp.minimum(jnp.searchsorted(pend, rho, side="right").astype(jnp.int32), N_EXPERTS - 1)
    off = rho - pstart[e_of]
    valid = jnp.logical_and(off < counts[e_of], rho < pend[-1])
    a = order[jnp.clip(start[e_of] + off, 0, n_assign - 1)]
    tok, slot = a // 2, a % 2
    row_src = jnp.where(valid, tok, 0).astype(jnp.int32)
    row_dst = jnp.where(valid, slot * tp + tok, 0).astype(jnp.int32)
    row_w = jnp.where(valid, wts.reshape(-1)[a], 0.0).astype(F32)[:, None]
    tile_nvalid = jnp.sum(valid.reshape(n_tiles, tm), axis=1, dtype=jnp.int32)
    n_used = (pend[-1] // tm).astype(jnp.int32)
    t_idx = jnp.minimum(jnp.arange(n_tiles, dtype=jnp.int32), jnp.maximum(n_used - 1, 0))
    tile_expert = e_of[t_idx * tm]
    return tile_expert, tile_nvalid, row_src, row_dst, row_w


def kernel(x_prompt, x_sample, state_conv, state_ssm, cache_k, cache_v, a_norm, a_w_in, a_conv_w, a_conv_b, a_dt_bias, a_a_log, a_d_skip, a_gate_norm, a_w_out, kv_norm, w_kv, b_norm, b_w_q, b_w_o, moe_norm, router_group_w, router_group_b, router_expert_w, router_expert_b, expert_w_gate, expert_w_up, expert_w_down, final_norm):
    n_pb, seq, d_model = x_prompt.shape
    n_sb, n_new, _ = x_sample.shape
    n_a = a_w_in.shape[0]
    depth = moe_norm.shape[0]
    d_inner = a_w_out.shape[1]
    conv_dim = a_conv_w.shape[2]
    gn = (conv_dim - d_inner) // 2
    n_heads = d_inner // SSM_HEAD_DIM
    d_att = w_kv.shape[1] // 2
    n_p, n_s = n_pb * seq, n_sb * n_new
    lcm = MM_ROW_TILE * TOKEN_TILE // math.gcd(MM_ROW_TILE, TOKEN_TILE)
    tp = -(-(n_p + n_s) // lcm) * lcm
    L = SSD_CHUNK
    assert n_heads <= V7X_LANES and n_new <= L and n_new >= CONV_W - 1 and seq % (max(DILATIONS) * 128) == 0

    def pad_rows(parts):
        rows = sum(p.shape[0] for p in parts)
        return jnp.concatenate(parts + [jnp.zeros((tp - rows,) + parts[0].shape[1:], parts[0].dtype)], axis=0)

    def lanes(v):
        return jnp.pad(v.astype(F32), ((0, 0), (0, V7X_LANES - n_heads)))[:, None, :]

    head_of_col = jnp.arange(d_inner, dtype=jnp.int32) // SSM_HEAD_DIM
    ssd_prm = {
        "conv_w": a_conv_w, "conv_b": a_conv_b[:, None, :],
        "dt_bias": lanes(a_dt_bias), "a_log": lanes(a_a_log),
        "d_skip": jnp.repeat(a_d_skip.astype(F32), SSM_HEAD_DIM, axis=1)[:, None, :],
        "gate_norm": a_gate_norm[:, None, :],
        "expand": (jnp.arange(V7X_LANES, dtype=jnp.int32)[:, None] == head_of_col[None, :]).astype(BF16),
    }
    w_dt = jnp.pad(a_w_in[:, :, 2 * d_inner + 2 * gn:], ((0, 0), (0, 0), (0, V7X_LANES - n_heads)))
    r_w = jnp.concatenate([router_group_w, router_expert_w.reshape(depth, d_model, N_EXPERTS)], axis=2)
    r_w = jnp.pad(r_w, ((0, 0), (0, 0), (0, V7X_LANES - r_w.shape[2])))
    r_b = jnp.concatenate([router_group_b, router_expert_b.reshape(depth, N_EXPERTS)], axis=1)
    r_b = jnp.pad(r_b, ((0, 0), (0, V7X_LANES - r_b.shape[1])))[:, None, :]

    h = pad_rows([x_prompt.reshape(n_p, d_model), x_sample.reshape(n_s, d_model)])
    (xn,) = add_norm(h, [], a_norm[0:1], BF16, emit_sum=False)
    conv_p, ssm_p, conv_s, ssm_s = [], [], [], []
    kv = None
    for layer in range(depth):
        if layer < n_a:
            zxbc = matmul(xn, a_w_in, layer, 2 * d_inner + 2 * gn, 1024)
            dt_raw = matmul(xn, w_dt, layer, V7X_LANES, V7X_LANES)
            y_p, cp, sp = ssd_mixer(zxbc, dt_raw, 0, n_pb, seq // L, L, None, None, ssd_prm, layer, d_inner, gn)
            zs = jnp.pad(zxbc[n_p:n_p + n_s].reshape(n_sb, n_new, -1), ((0, 0), (0, L - n_new), (0, 0)))
            ds = jnp.pad(dt_raw[n_p:n_p + n_s].reshape(n_sb, n_new, -1), ((0, 0), (0, L - n_new), (0, 0)))
            y_s, cs, ss = ssd_mixer(zs.reshape(n_sb * L, -1), ds.reshape(n_sb * L, -1), 0, n_sb, 1, n_new,
                                    state_conv[layer], state_ssm[layer].reshape(n_sb, n_heads * SSM_HEAD_DIM, D_STATE),
                                    ssd_prm, layer, d_inner, gn)
            conv_p.append(cp); ssm_p.append(sp); conv_s.append(cs); ssm_s.append(ss)
            y = pad_rows([y_p, y_s.reshape(n_sb, L, d_inner)[:, :n_new].reshape(n_s, d_inner)])
            h = matmul(y, a_w_out, layer, d_model, 512, res=h)
        else:
            b = layer - n_a
            q = matmul(xn, b_w_q, b, b_w_q.shape[2], 1024)
            o_p = attn_prompt(q, kv, n_pb, seq)
            qs = q[n_p:n_p + n_s].reshape(n_sb, n_new, len(DILATIONS), N_KV_HEADS, ATT_HEAD_DIM)
            qs = qs.transpose(0, 3, 2, 1, 4).reshape(n_sb, N_KV_HEADS, len(DILATIONS) * n_new, ATT_HEAD_DIM)
            qs = jnp.pad(qs, ((0, 0), (0, 0), (0, 16 - len(DILATIONS) * n_new), (0, 0)))
            kvs = kv[n_p:n_p + n_s].reshape(n_sb, n_new, 2, N_KV_HEADS, ATT_HEAD_DIM).transpose(2, 0, 3, 1, 4)
            kvs = jnp.pad(kvs, ((0, 0), (0, 0), (0, 0), (0, V7X_LANES - n_new), (0, 0)))
            o_s = attn_sample(qs, kvs[0], kvs[1], cache_k.reshape(n_sb, -1, d_att),
                              cache_v.reshape(n_sb, -1, d_att), n_new)
            o = pad_rows([o_p, o_s.transpose(0, 2, 1, 3).reshape(n_s, d_att)])
            h = matmul(o, b_w_o, b, d_model, 1024, res=h)
        xn32, idx, wts = router(h, moe_norm[layer:layer + 1], r_w[layer], r_b[layer])
        tile_expert, tile_nvalid, row_src, row_dst, row_w = moe_plan(idx, wts, tp)
        y_moe = expert_ffn(xn32, tile_expert, tile_nvalid, row_src, row_dst, row_w,
                           expert_w_gate, expert_w_up, expert_w_down, layer)
        adds = [(y_moe, 0), (y_moe, tp // TOKEN_TILE)]
        if layer + 1 < n_a:
            h, xn = add_norm(h, adds, a_norm[layer + 1:layer + 2], BF16, emit_sum=True)
        elif layer + 1 == n_a:
            h, xkv, xn = add_norm(h, adds, jnp.stack([kv_norm, b_norm[0]]), BF16, emit_sum=True)
            kv = matmul(xkv, w_kv, None, w_kv.shape[1], 1024)
        elif layer + 1 < depth:
            h, xn = add_norm(h, adds, b_norm[layer + 1 - n_a:layer + 2 - n_a], BF16, emit_sum=True)
        else:
            (y_out,) = add_norm(h, adds, final_norm[None, :], F32, emit_sum=False)

    y_prompt = y_out[:n_p].reshape(n_pb, seq, d_model)
    y_sample = y_out[n_p:n_p + n_s].reshape(n_sb, n_new, d_model)
    keep = min(max(DILATIONS) * (N_DIL_KEYS - 1), seq)
    kv_p = kv[:n_p].reshape(n_pb, seq, 2, N_KV_HEADS, ATT_HEAD_DIM)[:, seq - keep:]
    kv_s = kv[n_p:n_p + n_s].reshape(n_sb, n_new, 2, N_KV_HEADS, ATT_HEAD_DIM)
    ssm_shape = (n_heads, SSM_HEAD_DIM, D_STATE)
    return (y_prompt, y_sample,
            jnp.stack(conv_p), jnp.stack(ssm_p).reshape((n_a, n_pb) + ssm_shape),
            kv_p[:, :, 0], kv_p[:, :, 1],
            jnp.stack(conv_s), jnp.stack(ssm_s).reshape((n_a, n_sb) + ssm_shape),
            kv_s[:, :, 0], kv_s[:, :, 1])
```

```python
import functools
import math

import jax
import jax.numpy as jnp
from jax import lax
from jax.experimental import pallas as pl
from jax.experimental.pallas import tpu as pltpu

F32 = jnp.float32
BF16 = jnp.bfloat16

V7X_VMEM_BYTES = 64 * 1024 * 1024
V7X_LANES = 128
V7X_SUBLANES = 8
VMEM_LIMIT = 56 * 1024 * 1024

RMS_EPS = 1e-6
SSM_HEAD_DIM = 64
N_SSM_GROUPS = 8
D_STATE = 128
CONV_W = 4
SSD_CHUNK = 128
ATT_HEAD_DIM = 128
N_KV_HEADS = 8
DILATIONS = (1, 4, 16)
N_DIL_KEYS = 129
assert all(d & (d - 1) == 0 for d in DILATIONS)
N_EXPERT_GROUPS = 4
EXPERTS_PER_GROUP = 8
N_EXPERTS = N_EXPERT_GROUPS * EXPERTS_PER_GROUP
NEG_INF = -1e30

TOKEN_TILE = 256
MM_ROW_TILE = 768
MOE_ROW_TILE = 256


def _cparams(sem):
    return pltpu.CompilerParams(dimension_semantics=sem, vmem_limit_bytes=VMEM_LIMIT)


def _norm_kernel(*refs, n_add, n_out, emit_sum):
    h = refs[0][...]
    for a in refs[1:1 + n_add]:
        h = h + a[...]
    g_ref = refs[1 + n_add]
    outs = refs[2 + n_add:]
    if emit_sum:
        outs[0][...] = h
        outs = outs[1:]
    xn = h * lax.rsqrt(jnp.mean(h * h, axis=-1, keepdims=True) + RMS_EPS)
    for k in range(n_out):
        outs[k][...] = (xn * g_ref[k:k + 1, :]).astype(outs[k].dtype)


def add_norm(h, adds, gains, out_dtype, emit_sum):
    tp, d = h.shape
    n_out = gains.shape[0]
    tm = TOKEN_TILE
    in_specs = [pl.BlockSpec((tm, d), lambda i: (i, 0))]
    args = [h]
    for arr, off in adds:
        in_specs.append(pl.BlockSpec((tm, d), functools.partial(lambda i, o: (i + o, 0), o=off)))
        args.append(arr)
    in_specs.append(pl.BlockSpec((n_out, d), lambda i: (0, 0)))
    args.append(gains)
    out_shape, out_specs = [], []
    if emit_sum:
        out_shape.append(jax.ShapeDtypeStruct((tp, d), F32))
        out_specs.append(pl.BlockSpec((tm, d), lambda i: (i, 0)))
    for _ in range(n_out):
        out_shape.append(jax.ShapeDtypeStruct((tp, d), out_dtype))
        out_specs.append(pl.BlockSpec((tm, d), lambda i: (i, 0)))
    return pl.pallas_call(
        functools.partial(_norm_kernel, n_add=len(adds), n_out=n_out, emit_sum=emit_sum),
        grid=(tp // tm,), in_specs=in_specs, out_specs=out_specs, out_shape=out_shape,
        compiler_params=_cparams(("parallel",)), name="add_norm",
    )(*args)


def _mm_kernel(*refs, has_res):
    if has_res:
        x_ref, w_ref, r_ref, o_ref, wb_ref = refs
    else:
        x_ref, w_ref, o_ref, wb_ref = refs

    @pl.when(pl.program_id(1) == 0)
    def _():
        wb_ref[...] = w_ref[...].astype(BF16)

    acc = jnp.dot(x_ref[...].astype(BF16), wb_ref[...], preferred_element_type=F32)
    if has_res:
        acc = r_ref[...] + acc
    o_ref[...] = acc


def matmul(x, w, layer, n_cols, tn, res=None, tm=MM_ROW_TILE):
    tp, k = x.shape
    assert tp % tm == 0 and n_cols % tn == 0
    if layer is None:
        w_spec = pl.BlockSpec((k, tn), lambda j, i: (0, j))
    else:
        w_spec = pl.BlockSpec((None, k, tn), lambda j, i: (layer, 0, j))
    in_specs = [pl.BlockSpec((tm, k), lambda j, i: (i, 0)), w_spec]
    args = [x, w]
    if res is not None:
        in_specs.append(pl.BlockSpec((tm, tn), lambda j, i: (i, j)))
        args.append(res)
    return pl.pallas_call(
        functools.partial(_mm_kernel, has_res=res is not None),
        grid=(n_cols // tn, tp // tm), in_specs=in_specs,
        out_specs=pl.BlockSpec((tm, tn), lambda j, i: (i, j)),
        out_shape=jax.ShapeDtypeStruct((tp, n_cols), F32),
        scratch_shapes=[pltpu.VMEM((k, tn), BF16)],
        compiler_params=_cparams(("arbitrary", "arbitrary")), name="proj",
    )(*args)


def _silu(x):
    return x * (1.0 / (1.0 + jnp.exp(-x)))


def _split2(v):
    hi = v.astype(BF16)
    lo = (v - hi.astype(F32)).astype(BF16)
    return hi, lo


def _ssd_kernel(z_ref, x_ref, b_ref, c_ref, dt_ref, conv0_ref, ssm0_ref,
                cw_ref, cb_ref, dtb_ref, alog_ref, dsk_ref, gn_ref, e_ref,
                y_ref, convo_ref, ssmo_ref,
                xpad, xs_s, xlo_s, xhi_s, xw_s, bm_s, bmt_s, cm_s, y_s, ea_s, state_s,
                *, n_valid, has_init):
    L = SSD_CHUNK
    d_inner = xs_s.shape[1]
    gn = bm_s.shape[1]
    n_heads = d_inner // SSM_HEAD_DIM
    pair = 2 * SSM_HEAD_DIM
    gcols = d_inner // N_SSM_GROUPS
    c = pl.program_id(1)
    pad0 = V7X_SUBLANES
    keep = CONV_W - 1

    @pl.when(c == 0)
    def _():
        if has_init:
            xpad[pad0 - keep:pad0, :] = conv0_ref[...]
            for g in range(N_SSM_GROUPS):
                state_s[:, g * gcols:(g + 1) * gcols] = ssm0_ref[g * gcols:(g + 1) * gcols, :].T
        else:
            xpad[0:pad0, :] = jnp.zeros((pad0, xpad.shape[1]), F32)
            state_s[...] = jnp.zeros(state_s.shape, F32)

    xpad[pad0:pad0 + L, 0:d_inner] = x_ref[...]
    xpad[pad0:pad0 + L, d_inner:d_inner + gn] = b_ref[...]
    xpad[pad0:pad0 + L, d_inner + gn:d_inner + 2 * gn] = c_ref[...]
    lane = lax.broadcasted_iota(jnp.int32, (L, 512), 1)
    even_head = (lane // SSM_HEAD_DIM) % 2 == 0
    strip = 512
    for s in range(xpad.shape[1] // strip):
        cs = slice(s * strip, (s + 1) * strip)
        acc = jnp.broadcast_to(cb_ref[:, cs], (L, strip))
        for k in range(CONV_W):
            acc = acc + cw_ref[k:k + 1, cs] * xpad[pad0 - keep + k:pad0 - keep + k + L, cs]
        v = _silu(acc)
        if s * strip < d_inner:
            xs_s[:, cs] = v
            vb = v.astype(BF16)
            xlo_s[:, cs] = jnp.where(even_head, vb, jnp.zeros_like(vb))
            xhi_s[:, cs] = jnp.where(even_head, jnp.zeros_like(vb), vb)
        elif s * strip < d_inner + gn:
            o = s * strip - d_inner
            bm_s[:, o:o + strip] = v.astype(BF16)
            for q in range(strip // D_STATE):
                gi = o // D_STATE + q
                bmt_s[:, gi * L:(gi + 1) * L] = v[:, q * D_STATE:(q + 1) * D_STATE].T.astype(BF16)
        else:
            o = s * strip - d_inner - gn
            cm_s[:, o:o + strip] = v.astype(BF16)

    tail = xpad[pad0 + n_valid - keep:pad0 + n_valid, :]
    xpad[pad0 - keep:pad0, :] = tail

    @pl.when(c == pl.num_programs(1) - 1)
    def _():
        convo_ref[...] = tail

    row = lax.broadcasted_iota(jnp.int32, (L, V7X_LANES), 0)
    dtv = dt_ref[...] + dtb_ref[...]
    dt = jnp.maximum(dtv, 0.0) + jnp.log(1.0 + jnp.exp(-jnp.abs(dtv)))
    dt = jnp.where(row < n_valid, dt, 0.0)
    a = -jnp.exp(alog_ref[...])
    adt = dt * a
    ii = lax.broadcasted_iota(jnp.int32, (L, L), 0)
    jj = lax.broadcasted_iota(jnp.int32, (L, L), 1)
    causal = ii >= jj
    tri = jnp.where(causal, 1.0, 0.0).astype(F32)
    a_cs = jnp.dot(tri, adt, preferred_element_type=F32, precision=lax.Precision.HIGHEST)
    a_cs_t = a_cs.T
    dt_t = dt.T
    ea = jnp.exp(a_cs)
    dte = dt * jnp.exp(a_cs[L - 1:L, :] - a_cs)
    both = jnp.concatenate([ea, dte], axis=0)
    hi, lo = _split2(both)
    for s in range(d_inner // 512):
        cs = slice(s * 512, (s + 1) * 512)
        ex = (jnp.dot(hi, e_ref[:, cs], preferred_element_type=F32)
              + jnp.dot(lo, e_ref[:, cs], preferred_element_type=F32))
        ea_s[:, cs] = ex[0:L]
        xw_s[:, cs] = (xs_s[:, cs] * ex[L:2 * L]).astype(BF16)

    for g in range(N_SSM_GROUPS):
        gs = slice(g * gcols, (g + 1) * gcols)
        ns = slice(g * D_STATE, (g + 1) * D_STATE)
        cg = cm_s[:, ns]
        cb = lax.dot_general(cg, bm_s[:, ns], (((1,), (1,)), ((), ())), preferred_element_type=F32)
        st = state_s[:, gs]
        y_off = jnp.dot(cg, st.astype(BF16), preferred_element_type=F32)
        for q in range(gcols // pair):
            col = slice(g * gcols + q * pair, g * gcols + (q + 1) * pair)
            acc = None
            for hh, xsrc in ((0, xlo_s), (1, xhi_s)):
                h = (g * gcols + q * pair) // SSM_HEAD_DIM + hh
                seg = a_cs[:, h:h + 1] - a_cs_t[h:h + 1, :]
                dec = jnp.exp(jnp.where(causal, seg, NEG_INF))
                m = (cb * dec * dt_t[h:h + 1, :]).astype(BF16)
                part = jnp.dot(m, xsrc[:, col], preferred_element_type=F32)
                acc = part if acc is None else acc + part
            lc = slice(q * pair, (q + 1) * pair)
            y_s[:, col] = acc + y_off[:, lc] * ea_s[:, col] + dsk_ref[:, col] * xs_s[:, col]
        new = st * ea_s[L - 1:L, gs] + jnp.dot(bmt_s[:, g * L:(g + 1) * L], xw_s[:, gs],
                                               preferred_element_type=F32)
        state_s[:, gs] = new

    for g in range(N_SSM_GROUPS):
        gs = slice(g * gcols, (g + 1) * gcols)
        v = y_s[:, gs] * _silu(z_ref[:, gs])
        v = v * lax.rsqrt(jnp.mean(v * v, axis=-1, keepdims=True) + RMS_EPS)
        y_ref[:, gs] = (v * gn_ref[:, gs]).astype(y_ref.dtype)

    @pl.when(c == pl.num_programs(1) - 1)
    def _():
        for g in range(N_SSM_GROUPS):
            ssmo_ref[g * gcols:(g + 1) * gcols, :] = state_s[:, g * gcols:(g + 1) * gcols].T


def ssd_mixer(zxbc, dt_raw, row0, n_seq, n_chunks, n_valid, conv0, ssm0, prm, layer, d_inner, gn):
    L = SSD_CHUNK
    assert row0 % L == 0
    rb = row0 // L
    conv_dim = d_inner + 2 * gn
    has_init = conv0 is not None
    if not has_init:
        conv0 = jnp.zeros((n_seq, CONV_W - 1, conv_dim), F32)
        ssm0 = jnp.zeros((n_seq, 8, D_STATE), F32)
    n_heads = d_inner // SSM_HEAD_DIM
    rows = n_seq * n_chunks * L

    def rmap(b, c):
        return rb + b * n_chunks + c

    zb = d_inner // gn
    in_specs = [
        pl.BlockSpec((L, d_inner), lambda b, c: (rmap(b, c), 0)),
        pl.BlockSpec((L, d_inner), lambda b, c: (rmap(b, c), 1)),
        pl.BlockSpec((L, gn), lambda b, c: (rmap(b, c), 2 * zb)),
        pl.BlockSpec((L, gn), lambda b, c: (rmap(b, c), 2 * zb + 1)),
        pl.BlockSpec((L, V7X_LANES), lambda b, c: (rmap(b, c), 0)),
        pl.BlockSpec((None, CONV_W - 1, conv_dim), lambda b, c: (b, 0, 0)),
        pl.BlockSpec((None,) + ssm0.shape[1:], lambda b, c: (b, 0, 0)),
        pl.BlockSpec((None, CONV_W, conv_dim), lambda b, c: (layer, 0, 0)),
        pl.BlockSpec((None, 1, conv_dim), lambda b, c: (layer, 0, 0)),
        pl.BlockSpec((None, 1, V7X_LANES), lambda b, c: (layer, 0, 0)),
        pl.BlockSpec((None, 1, V7X_LANES), lambda b, c: (layer, 0, 0)),
        pl.BlockSpec((None, 1, d_inner), lambda b, c: (layer, 0, 0)),
        pl.BlockSpec((None, 1, d_inner), lambda b, c: (layer, 0, 0)),
        pl.BlockSpec((V7X_LANES, d_inner), lambda b, c: (0, 0)),
    ]
    out_shape = [
        jax.ShapeDtypeStruct((rows, d_inner), BF16),
        jax.ShapeDtypeStruct((n_seq, CONV_W - 1, conv_dim), F32),
        jax.ShapeDtypeStruct((n_seq, n_heads * SSM_HEAD_DIM, D_STATE), F32),
    ]
    out_specs = [
        pl.BlockSpec((L, d_inner), lambda b, c: (b * n_chunks + c, 0)),
        pl.BlockSpec((None, CONV_W - 1, conv_dim), lambda b, c: (b, 0, 0)),
        pl.BlockSpec((None, n_heads * SSM_HEAD_DIM, D_STATE), lambda b, c: (b, 0, 0)),
    ]
    scratch = [
        pltpu.VMEM((L + V7X_SUBLANES, conv_dim), F32),
        pltpu.VMEM((L, d_inner), F32),
        pltpu.VMEM((L, d_inner), BF16),
        pltpu.VMEM((L, d_inner), BF16),
        pltpu.VMEM((L, d_inner), BF16),
        pltpu.VMEM((L, gn), BF16),
        pltpu.VMEM((D_STATE, N_SSM_GROUPS * L), BF16),
        pltpu.VMEM((L, gn), BF16),
        pltpu.VMEM((L, d_inner), F32),
        pltpu.VMEM((L, d_inner), F32),
        pltpu.VMEM((D_STATE, d_inner), F32),
    ]
    return pl.pallas_call(
        functools.partial(_ssd_kernel, n_valid=n_valid, has_init=has_init),
        grid=(n_seq, n_chunks), in_specs=in_specs, out_specs=out_specs, out_shape=out_shape,
        scratch_shapes=scratch,
        compiler_params=_cparams(("arbitrary", "arbitrary")), name="ssd_mixer",
    )(zxbc, zxbc, zxbc, zxbc, dt_raw, conv0, ssm0,
      prm["conv_w"], prm["conv_b"], prm["dt_bias"], prm["a_log"], prm["d_skip"], prm["gate_norm"],
      prm["expand"])


def _attn_prompt_kernel(q0_ref, q1_ref, q2_ref, k_ref, v_ref, o_ref, acc_s, m_s, l_s, *, seq):
    QB = 128
    scale = ATT_HEAD_DIM ** -0.5
    qq = lax.broadcasted_iota(jnp.int32, (QB, QB), 0)
    kk = lax.broadcasted_iota(jnp.int32, (QB, QB), 1)
    cur_ok = kk <= qq
    kq = kk - qq

    def scores(q, k_rows, ok):
        s = lax.dot_general(q, k_rows.astype(BF16), (((1,), (1,)), ((), ())),
                            preferred_element_type=F32) * scale
        return jnp.where(ok, s, NEG_INF)

    for g, (q_ref, d) in enumerate(zip((q0_ref, q1_ref, q2_ref), DILATIONS)):
        nb_r = seq // d // QB
        first, last = g == 0, g == len(DILATIONS) - 1

        def body(idx, carry, q_ref=q_ref, d=d, nb_r=nb_r, first=first, last=last):
            r = idx // nb_r
            mb = idx % nb_r
            start = r + d * QB * mb
            pstart = jnp.maximum(start - d * QB, r)
            prev_min = jnp.where(mb > 0, 0, QB)
            rows = pl.ds(start, QB, stride=d) if d > 1 else pl.ds(start, QB)
            prows = pl.ds(pstart, QB, stride=d) if d > 1 else pl.ds(pstart, QB)
            q = q_ref[rows, :].astype(BF16)
            s_c = scores(q, k_ref[rows, :], cur_ok)
            s_p = scores(q, k_ref[prows, :], kq >= prev_min)
            m_blk = jnp.maximum(jnp.max(s_c, axis=-1, keepdims=True), jnp.max(s_p, axis=-1, keepdims=True))
            if first:
                m_new = jnp.broadcast_to(m_blk, (QB, QB))
            else:
                m_old = m_s[rows, :]
                m_new = jnp.maximum(m_old, m_blk)
            p_c = jnp.exp(s_c - m_new)
            p_p = jnp.exp(s_p - m_new)
            l_blk = jnp.sum(p_c, axis=-1, keepdims=True) + jnp.sum(p_p, axis=-1, keepdims=True)
            pv = (jnp.dot(p_c.astype(BF16), v_ref[rows, :].astype(BF16), preferred_element_type=F32)
                  + jnp.dot(p_p.astype(BF16), v_ref[prows, :].astype(BF16), preferred_element_type=F32))
            if first:
                l_new = jnp.broadcast_to(l_blk, (QB, QB))
                acc = pv
            else:
                alpha = jnp.exp(m_old - m_new)
                l_new = alpha * l_s[rows, :] + l_blk
                acc = alpha * acc_s[rows, :] + pv
            if last:
                o_ref[rows, :] = acc / l_new
            else:
                m_s[rows, :] = m_new
                l_s[rows, :] = l_new
                acc_s[rows, :] = acc
            return carry

        lax.fori_loop(0, seq // QB, body, 0)


def attn_prompt(q, kv, n_seq, seq):
    hd = ATT_HEAD_DIM
    nh = N_KV_HEADS

    def qspec(g):
        return pl.BlockSpec((seq, hd), lambda b, h: (b, g * nh + h))

    return pl.pallas_call(
        functools.partial(_attn_prompt_kernel, seq=seq),
        grid=(n_seq, nh),
        in_specs=[qspec(0), qspec(1), qspec(2),
                  pl.BlockSpec((seq, hd), lambda b, h: (b, h)),
                  pl.BlockSpec((seq, hd), lambda b, h: (b, nh + h))],
        out_specs=pl.BlockSpec((seq, hd), lambda b, h: (b, h)),
        out_shape=jax.ShapeDtypeStruct((n_seq * seq, nh * hd), F32),
        scratch_shapes=[pltpu.VMEM((seq, hd), F32)] * 3,
        compiler_params=_cparams(("parallel", "parallel")), name="attn_prompt",
    )(q, q, q, kv, kv)


def _attn_sample_kernel(q_ref, kn_ref, vn_ref, kc_ref, vc_ref, o_ref, *, n_new, past):
    QR = q_ref.shape[0]
    scale = ATT_HEAD_DIM ** -0.5
    q = q_ref[...].astype(BF16)

    def dist_ok(dist, dil):
        return jnp.logical_and(jnp.logical_and(dist >= 0, dist <= dil * (N_DIL_KEYS - 1)),
                               jnp.bitwise_and(dist, dil - 1) == 0)

    def mask(n_keys, key0):
        rr = lax.broadcasted_iota(jnp.int32, (QR, n_keys), 0)
        kpos = lax.broadcasted_iota(jnp.int32, (QR, n_keys), 1) + key0
        g = rr // n_new
        qpos = past + rr % n_new
        dil = jnp.where(g == 0, DILATIONS[0], jnp.where(g == 1, DILATIONS[1], DILATIONS[2]))
        ok = dist_ok(qpos - kpos, dil)
        return jnp.logical_and(ok, g < len(DILATIONS))

    ok_c = mask(past, 0)
    ok_n = jnp.logical_and(mask(kn_ref.shape[0], past),
                           lax.broadcasted_iota(jnp.int32, (QR, kn_ref.shape[0]), 1) < n_new)
    s_c = lax.dot_general(q, kc_ref[...].astype(BF16), (((1,), (1,)), ((), ())),
                          preferred_element_type=F32) * scale
    s_n = lax.dot_general(q, kn_ref[...].astype(BF16), (((1,), (1,)), ((), ())),
                          preferred_element_type=F32) * scale
    s_c = jnp.where(ok_c, s_c, NEG_INF)
    s_n = jnp.where(ok_n, s_n, NEG_INF)
    m = jnp.maximum(jnp.max(s_c, axis=-1, keepdims=True), jnp.max(s_n, axis=-1, keepdims=True))
    p_c = jnp.exp(s_c - m)
    p_n = jnp.exp(s_n - m)
    l = jnp.sum(p_c, axis=-1, keepdims=True) + jnp.sum(p_n, axis=-1, keepdims=True)
    pv = (jnp.dot(p_c.astype(BF16), vc_ref[...].astype(BF16), preferred_element_type=F32)
          + jnp.dot(p_n.astype(BF16), vn_ref[...].astype(BF16), preferred_element_type=F32))
    ms = [m[g * n_new:(g + 1) * n_new] for g in range(len(DILATIONS))]
    m_all = functools.reduce(jnp.maximum, ms)
    num = jnp.zeros((n_new, ATT_HEAD_DIM), F32)
    den = jnp.zeros((n_new, 1), F32)
    for g in range(len(DILATIONS)):
        w = jnp.exp(ms[g] - m_all)
        num = num + w * pv[g * n_new:(g + 1) * n_new]
        den = den + w * l[g * n_new:(g + 1) * n_new]
    o_ref[...] = num / den


def attn_sample(q_s, k_new, v_new, cache_k, cache_v, n_new):
    nb, nh, qr, hd = q_s.shape
    past = cache_k.shape[1]
    return pl.pallas_call(
        functools.partial(_attn_sample_kernel, n_new=n_new, past=past),
        grid=(nb, nh),
        in_specs=[pl.BlockSpec((None, None, qr, hd), lambda b, h: (b, h, 0, 0)),
                  pl.BlockSpec((None, None, k_new.shape[2], hd), lambda b, h: (b, h, 0, 0)),
                  pl.BlockSpec((None, None, k_new.shape[2], hd), lambda b, h: (b, h, 0, 0)),
                  pl.BlockSpec((None, past, hd), lambda b, h: (b, 0, h)),
                  pl.BlockSpec((None, past, hd), lambda b, h: (b, 0, h))],
        out_specs=pl.BlockSpec((None, None, n_new, hd), lambda b, h: (b, h, 0, 0)),
        out_shape=jax.ShapeDtypeStruct((nb, nh, n_new, hd), F32),
        compiler_params=_cparams(("parallel", "parallel")), name="attn_sample",
    )(q_s, k_new, v_new, cache_k, cache_v)


def _router_kernel(h_ref, g_ref, w_ref, b_ref, xn_ref, idx_ref, wt_ref):
    h = h_ref[...]
    xn = h * lax.rsqrt(jnp.mean(h * h, axis=-1, keepdims=True) + RMS_EPS) * g_ref[...]
    xn_ref[...] = xn.astype(xn_ref.dtype)
    logits = jnp.dot(xn, w_ref[...], preferred_element_type=F32,
                     precision=lax.Precision.HIGHEST) + b_ref[...]
    tm = logits.shape[0]
    lane = lax.broadcasted_iota(jnp.int32, logits.shape, 1)
    big = jnp.int32(V7X_LANES)
    is_g = lane < N_EXPERT_GROUPS
    gl = jnp.where(is_g, logits, NEG_INF)
    gmax = jnp.max(gl, axis=-1, keepdims=True)
    g_idx = jnp.min(jnp.where(jnp.logical_and(is_g, gl == gmax), lane, big), axis=-1, keepdims=True)
    g_w = 1.0 / jnp.sum(jnp.where(is_g, jnp.exp(gl - gmax), 0.0), axis=-1, keepdims=True)
    lo = N_EXPERT_GROUPS + g_idx * EXPERTS_PER_GROUP
    in_grp = jnp.logical_and(lane >= lo, lane < lo + EXPERTS_PER_GROUP)
    el = jnp.where(in_grp, logits, NEG_INF)
    v1 = jnp.max(el, axis=-1, keepdims=True)
    i1 = jnp.min(jnp.where(jnp.logical_and(in_grp, el == v1), lane, big), axis=-1, keepdims=True)
    el2 = jnp.where(lane == i1, NEG_INF, el)
    v2 = jnp.max(el2, axis=-1, keepdims=True)
    i2 = jnp.min(jnp.where(jnp.logical_and(in_grp, jnp.logical_and(el2 == v2, lane != i1)), lane, big),
                 axis=-1, keepdims=True)
    e2 = jnp.exp(v2 - v1)
    w1 = g_w / (1.0 + e2)
    w2 = g_w * e2 / (1.0 + e2)
    col = lax.broadcasted_iota(jnp.int32, (tm, 2), 1)
    idx_ref[...] = jnp.where(col == 0, i1, i2) - N_EXPERT_GROUPS
    wt_ref[...] = jnp.where(col == 0, w1, w2)


def router(h, gain, w_cat, b_cat):
    tp, d = h.shape
    tm = TOKEN_TILE
    return pl.pallas_call(
        _router_kernel, grid=(tp // tm,),
        in_specs=[pl.BlockSpec((tm, d), lambda i: (i, 0)),
                  pl.BlockSpec((1, d), lambda i: (0, 0)),
                  pl.BlockSpec((d, V7X_LANES), lambda i: (0, 0)),
                  pl.BlockSpec((1, V7X_LANES), lambda i: (0, 0))],
        out_specs=[pl.BlockSpec((tm, d), lambda i: (i, 0)),
                   pl.BlockSpec((tm, 2), lambda i: (i, 0)),
                   pl.BlockSpec((tm, 2), lambda i: (i, 0))],
        out_shape=[jax.ShapeDtypeStruct((tp, d), BF16),
                   jax.ShapeDtypeStruct((tp, 2), jnp.int32),
                   jax.ShapeDtypeStruct((tp, 2), F32)],
        compiler_params=_cparams(("parallel",)), name="router",
    )(h, gain, w_cat, b_cat)


RUN_ALIGN = 16
RUN_CHUNK = 64
LOCAL_ROWS = 2 * TOKEN_TILE + N_EXPERTS * RUN_ALIGN


def _for_pieces(n, chunk, fn):
    n_big = n // chunk
    lax.fori_loop(0, n_big, lambda j, c: (fn(j * chunk, chunk), c)[1], 0)
    done = n_big * chunk
    size = chunk // 2
    while size >= RUN_ALIGN:
        hit = (n & size) != 0
        pl.when(hit)(functools.partial(fn, done, size))
        done = done + jnp.where(hit, size, 0)
        size //= 2


def _for_run_pieces(plen_ref, loff_ref, gdst_ref, blk, fn):
    def per_run(e, carry):
        k = blk * N_EXPERTS + e
        lo, gd = loff_ref[k], gdst_ref[k]
        _for_pieces(plen_ref[k], RUN_CHUNK,
                    lambda off, size: fn(pl.multiple_of(lo + off, RUN_ALIGN),
                                         pl.multiple_of(gd + off, RUN_ALIGN), size))
        return carry
    lax.fori_loop(0, N_EXPERTS, per_run, 0)


def _dispatch_kernel(plen_ref, loff_ref, gdst_ref, fill0_ref, filln_ref,
                     x_ref, src_ref, xs_hbm, loc, zeros, sem, zsem):
    b = pl.program_id(0)
    nb = pl.num_programs(0)
    slot = b % 2

    def copy(s, l, g, size):
        return pltpu.make_async_copy(loc.at[s, pl.ds(l, size)], xs_hbm.at[pl.ds(g, size)], sem.at[s])

    def wait_block(blk, s):
        _for_run_pieces(plen_ref, loff_ref, gdst_ref, blk, lambda l, g, size: copy(s, l, g, size).wait())

    @pl.when(b >= 2)
    def _():
        wait_block(b - 2, slot)

    tok = lax.broadcasted_iota(jnp.int32, (LOCAL_ROWS, TOKEN_TILE), 1)
    pick = jnp.where(jnp.right_shift(src_ref[...], 1) == tok, 1.0, 0.0).astype(BF16)
    loc[slot] = jnp.dot(pick, x_ref[...], preferred_element_type=F32).astype(BF16)
    _for_run_pieces(plen_ref, loff_ref, gdst_ref, b, lambda l, g, size: copy(slot, l, g, size).start())

    @pl.when(b == nb - 1)
    def _():
        @pl.when(b >= 1)
        def _():
            wait_block(b - 1, 1 - slot)
        wait_block(b, slot)
        zeros[...] = jnp.zeros(zeros.shape, zeros.dtype)

        def fill(op):
            def per_expert(e, carry):
                f0 = fill0_ref[e]
                _for_pieces(filln_ref[e], MOE_ROW_TILE // 2, lambda off, size: op(pltpu.make_async_copy(
                    zeros.at[pl.ds(0, size)], xs_hbm.at[pl.ds(pl.multiple_of(f0 + off, RUN_ALIGN), size)], zsem)))
                return carry
            lax.fori_loop(0, fill0_ref.shape[0], per_expert, 0)

        fill(lambda c: c.start())
        fill(lambda c: c.wait())


def moe_dispatch(xn, plan):
    tp, d = xn.shape
    nb = tp // TOKEN_TILE
    grid_spec = pltpu.PrefetchScalarGridSpec(
        num_scalar_prefetch=5, grid=(nb,),
        in_specs=[pl.BlockSpec((TOKEN_TILE, d), lambda b, *_: (b, 0)),
                  pl.BlockSpec((None, LOCAL_ROWS, 1), lambda b, *_: (b, 0, 0))],
        out_specs=pl.BlockSpec(memory_space=pl.ANY),
        scratch_shapes=[pltpu.VMEM((2, LOCAL_ROWS, d), BF16), pltpu.VMEM((MOE_ROW_TILE // 2, d), BF16),
                        pltpu.SemaphoreType.DMA((2,)), pltpu.SemaphoreType.DMA],
    )
    return pl.pallas_call(
        _dispatch_kernel, grid_spec=grid_spec,
        out_shape=jax.ShapeDtypeStruct((plan["n_rows"], d), BF16),
        compiler_params=_cparams(("arbitrary",)), name="moe_dispatch",
    )(plan["plen"], plan["loff"], plan["gdst"], plan["fill0"], plan["filln"], xn, plan["src_col"])


def _expert_kernel(te_ref, nu_ref, xs_ref, wg_ref, wu_ref, wd_ref, ys_ref, wgb, wub, wdb):
    i = pl.program_id(0)
    used = i < nu_ref[0]

    @pl.when(used)
    def _():
        prev = te_ref[jnp.maximum(i - 1, 0)]

        @pl.when(jnp.logical_or(i == 0, te_ref[i] != prev))
        def _():
            wgb[...] = wg_ref[...].astype(BF16)
            wub[...] = wu_ref[...].astype(BF16)
            wdb[...] = wd_ref[...].astype(BF16)

        x = xs_ref[...]
        hg = jnp.dot(x, wgb[...], preferred_element_type=F32)
        hu = jnp.dot(x, wub[...], preferred_element_type=F32)
        act = (_silu(hg) * hu).astype(BF16)
        ys_ref[...] = jnp.dot(act, wdb[...], preferred_element_type=F32).astype(ys_ref.dtype)

    @pl.when(jnp.logical_not(used))
    def _():
        ys_ref[...] = jnp.zeros(ys_ref.shape, ys_ref.dtype)


def expert_ffn(xs, plan, w_gate, w_up, w_down, layer):
    n_rows, d = xs.shape
    tm = MOE_ROW_TILE
    de = w_gate.shape[-1]

    def wmap(i, te, nu):
        return (layer, te[i], 0, 0)

    grid_spec = pltpu.PrefetchScalarGridSpec(
        num_scalar_prefetch=2, grid=(n_rows // tm,),
        in_specs=[pl.BlockSpec((tm, d), lambda i, te, nu: (jnp.minimum(i, nu[0] - 1), 0)),
                  pl.BlockSpec((None, None, d, de), wmap),
                  pl.BlockSpec((None, None, d, de), wmap),
                  pl.BlockSpec((None, None, de, d), wmap)],
        out_specs=pl.BlockSpec((tm, d), lambda i, te, nu: (i, 0)),
        scratch_shapes=[pltpu.VMEM((d, de), BF16), pltpu.VMEM((d, de), BF16), pltpu.VMEM((de, d), BF16)],
    )
    return pl.pallas_call(
        _expert_kernel, grid_spec=grid_spec,
        out_shape=jax.ShapeDtypeStruct((n_rows, d), BF16),
        compiler_params=_cparams(("arbitrary",)), name="expert_ffn",
    )(plan["tile_expert"], plan["n_used"], xs, w_gate, w_up, w_down)


def _combine_kernel(plen_ref, loff_ref, gdst_ref,
                    ys_hbm, src_ref, wt_ref, h_ref, g_ref, *rest, n_out, emit_sum):
    outs, (buf, sem) = rest[:-2], rest[-2:]
    b = pl.program_id(0)
    nb = pl.num_programs(0)
    slot = b % 2

    def copy(s, l, g, size):
        return pltpu.make_async_copy(ys_hbm.at[pl.ds(g, size)], buf.at[s, pl.ds(l, size)], sem.at[s])

    def fetch(blk, s, op):
        _for_run_pieces(plen_ref, loff_ref, gdst_ref, blk, lambda l, g, size: op(copy(s, l, g, size)))

    @pl.when(b == 0)
    def _():
        buf[...] = jnp.zeros(buf.shape, buf.dtype)
        fetch(0, 0, lambda c: c.start())

    @pl.when(b + 1 < nb)
    def _():
        fetch(b + 1, 1 - slot, lambda c: c.start())

    fetch(b, slot, lambda c: c.wait())
    y = buf[slot]
    tok2 = 2 * lax.broadcasted_iota(jnp.int32, (TOKEN_TILE, LOCAL_ROWS), 0)
    src = src_ref[...]
    h = h_ref[...]
    for k in range(2):
        pick = jnp.where(src == tok2 + k, 1.0, 0.0).astype(BF16)
        h = h + wt_ref[:, k:k + 1] * jnp.dot(pick, y, preferred_element_type=F32)
    if emit_sum:
        outs[0][...] = h
        outs = outs[1:]
    xn = h * lax.rsqrt(jnp.mean(h * h, axis=-1, keepdims=True) + RMS_EPS)
    for k in range(n_out):
        outs[k][...] = (xn * g_ref[k:k + 1, :]).astype(outs[k].dtype)


def moe_combine_norm(ys, plan, wts, h, gains, out_dtype, emit_sum):
    tp, d = h.shape
    nb = tp // TOKEN_TILE
    n_out = gains.shape[0]
    tok_spec = pl.BlockSpec((TOKEN_TILE, d), lambda b, *_: (b, 0))
    n_res = n_out + (1 if emit_sum else 0)
    grid_spec = pltpu.PrefetchScalarGridSpec(
        num_scalar_prefetch=3, grid=(nb,),
        in_specs=[pl.BlockSpec(memory_space=pl.ANY),
                  pl.BlockSpec((None, 1, LOCAL_ROWS), lambda b, *_: (b, 0, 0)),
                  pl.BlockSpec((TOKEN_TILE, 2), lambda b, *_: (b, 0)),
                  tok_spec,
                  pl.BlockSpec((n_out, d), lambda b, *_: (0, 0))],
        out_specs=[tok_spec] * n_res,
        scratch_shapes=[pltpu.VMEM((2, LOCAL_ROWS, d), BF16), pltpu.SemaphoreType.DMA((2,))],
    )
    out_shape = ([jax.ShapeDtypeStruct((tp, d), F32)] if emit_sum else []) + [
        jax.ShapeDtypeStruct((tp, d), out_dtype)] * n_out
    return pl.pallas_call(
        functools.partial(_combine_kernel, n_out=n_out, emit_sum=emit_sum), grid_spec=grid_spec,
        out_shape=out_shape, compiler_params=_cparams(("arbitrary",)), name="moe_combine_norm",
    )(plan["plen"], plan["loff"], plan["gdst"], ys, plan["src_row"], wts, h, gains)


def moe_plan(idx, tp):
    i32 = jnp.int32
    tm = MOE_ROW_TILE
    nb = tp // TOKEN_TILE
    n_asg = 2 * TOKEN_TILE
    n_tiles = -(-(2 * tp + nb * N_EXPERTS * (RUN_ALIGN - 1) + N_EXPERTS * (tm - 1)) // tm)
    e_blk = idx.reshape(nb, n_asg)
    onehot = (e_blk[:, :, None] == jnp.arange(N_EXPERTS, dtype=i32)).astype(i32)
    csum = jnp.cumsum(onehot, axis=1)
    rank = jnp.sum(onehot * csum, axis=2) - 1
    plen = (csum[:, -1, :] + RUN_ALIGN - 1) // RUN_ALIGN * RUN_ALIGN
    loff = jnp.cumsum(plen, axis=1) - plen
    lpos = jnp.sum(onehot * loff[:, None, :], axis=2) + rank
    hit = lpos[:, :, None] == jnp.arange(LOCAL_ROWS, dtype=i32)
    src = jnp.sum(jnp.where(hit, jnp.arange(1, n_asg + 1, dtype=i32)[None, :, None], 0), axis=1) - 1
    seg = jnp.sum(plen, axis=0)
    seg_pad = (seg + tm - 1) // tm * tm
    gend = jnp.cumsum(seg_pad)
    gstart = gend - seg_pad
    gdst = gstart[None, :] + jnp.cumsum(plen, axis=0) - plen
    n_used = gend[-1] // tm
    t_row = jnp.minimum(jnp.arange(n_tiles, dtype=i32), n_used - 1) * tm
    tile_expert = jnp.minimum(jnp.sum(gend[None, :] <= t_row[:, None], axis=1), N_EXPERTS - 1)
    return {
        "plen": plen.reshape(-1).astype(i32), "loff": loff.reshape(-1).astype(i32),
        "gdst": gdst.reshape(-1).astype(i32),
        "fill0": jnp.concatenate([gstart + seg, gend[-1:]]).astype(i32),
        "filln": jnp.concatenate([seg_pad - seg, n_tiles * tm - gend[-1:]]).astype(i32),
        "src_col": src[:, :, None].astype(i32), "src_row": src[:, None, :].astype(i32),
        "tile_expert": tile_expert.astype(i32), "n_used": n_used.reshape(1).astype(i32),
        "n_rows": n_tiles * tm,
    }


def kernel(x_prompt, x_sample, state_conv, state_ssm, cache_k, cache_v, a_norm, a_w_in, a_conv_w, a_conv_b, a_dt_bias, a_a_log, a_d_skip, a_gate_norm, a_w_out, kv_norm, w_kv, b_norm, b_w_q, b_w_o, moe_norm, router_group_w, router_group_b, router_expert_w, router_expert_b, expert_w_gate, expert_w_up, expert_w_down, final_norm):
    n_pb, seq, d_model = x_prompt.shape
    n_sb, n_new, _ = x_sample.shape
    n_a = a_w_in.shape[0]
    depth = moe_norm.shape[0]
    d_inner = a_w_out.shape[1]
    conv_dim = a_conv_w.shape[2]
    gn = (conv_dim - d_inner) // 2
    n_heads = d_inner // SSM_HEAD_DIM
    d_att = w_kv.shape[1] // 2
    n_p, n_s = n_pb * seq, n_sb * n_new
    lcm = MM_ROW_TILE * TOKEN_TILE // math.gcd(MM_ROW_TILE, TOKEN_TILE)
    tp = -(-(n_p + n_s) // lcm) * lcm
    L = SSD_CHUNK
    assert n_heads <= V7X_LANES and n_new <= L and n_new >= CONV_W - 1 and seq % (max(DILATIONS) * 128) == 0

    def pad_rows(parts):
        rows = sum(p.shape[0] for p in parts)
        return jnp.concatenate(parts + [jnp.zeros((tp - rows,) + parts[0].shape[1:], parts[0].dtype)], axis=0)

    def lanes(v):
        return jnp.pad(v.astype(F32), ((0, 0), (0, V7X_LANES - n_heads)))[:, None, :]

    head_of_col = jnp.arange(d_inner, dtype=jnp.int32) // SSM_HEAD_DIM
    ssd_prm = {
        "conv_w": a_conv_w, "conv_b": a_conv_b[:, None, :],
        "dt_bias": lanes(a_dt_bias), "a_log": lanes(a_a_log),
        "d_skip": jnp.repeat(a_d_skip.astype(F32), SSM_HEAD_DIM, axis=1)[:, None, :],
        "gate_norm": a_gate_norm[:, None, :],
        "expand": (jnp.arange(V7X_LANES, dtype=jnp.int32)[:, None] == head_of_col[None, :]).astype(BF16),
    }
    w_dt = jnp.pad(a_w_in[:, :, 2 * d_inner + 2 * gn:], ((0, 0), (0, 0), (0, V7X_LANES - n_heads)))
    r_w = jnp.concatenate([router_group_w, router_expert_w.reshape(depth, d_model, N_EXPERTS)], axis=2)
    r_w = jnp.pad(r_w, ((0, 0), (0, 0), (0, V7X_LANES - r_w.shape[2])))
    r_b = jnp.concatenate([router_group_b, router_expert_b.reshape(depth, N_EXPERTS)], axis=1)
    r_b = jnp.pad(r_b, ((0, 0), (0, V7X_LANES - r_b.shape[1])))[:, None, :]

    h = pad_rows([x_prompt.reshape(n_p, d_model), x_sample.reshape(n_s, d_model)])
    (xn,) = add_norm(h, [], a_norm[0:1], BF16, emit_sum=False)
    conv_p, ssm_p, conv_s, ssm_s = [], [], [], []
    kv = None
    for layer in range(depth):
        if layer < n_a:
            zxbc = matmul(xn, a_w_in, layer, 2 * d_inner + 2 * gn, 1024)
            dt_raw = matmul(xn, w_dt, layer, V7X_LANES, V7X_LANES)
            y_p, cp, sp = ssd_mixer(zxbc, dt_raw, 0, n_pb, seq // L, L, None, None, ssd_prm, layer, d_inner, gn)
            zs = jnp.pad(zxbc[n_p:n_p + n_s].reshape(n_sb, n_new, -1), ((0, 0), (0, L - n_new), (0, 0)))
            ds = jnp.pad(dt_raw[n_p:n_p + n_s].reshape(n_sb, n_new, -1), ((0, 0), (0, L - n_new), (0, 0)))
            y_s, cs, ss = ssd_mixer(zs.reshape(n_sb * L, -1), ds.reshape(n_sb * L, -1), 0, n_sb, 1, n_new,
                                    state_conv[layer], state_ssm[layer].reshape(n_sb, n_heads * SSM_HEAD_DIM, D_STATE),
                                    ssd_prm, layer, d_inner, gn)
            conv_p.append(cp); ssm_p.append(sp); conv_s.append(cs); ssm_s.append(ss)
            y = pad_rows([y_p, y_s.reshape(n_sb, L, d_inner)[:, :n_new].reshape(n_s, d_inner)])
            h = matmul(y, a_w_out, layer, d_model, 512, res=h)
        else:
            b = layer - n_a
            q = matmul(xn, b_w_q, b, b_w_q.shape[2], 1024)
            o_p = attn_prompt(q, kv, n_pb, seq)
            qs = q[n_p:n_p + n_s].reshape(n_sb, n_new, len(DILATIONS), N_KV_HEADS, ATT_HEAD_DIM)
            qs = qs.transpose(0, 3, 2, 1, 4).reshape(n_sb, N_KV_HEADS, len(DILATIONS) * n_new, ATT_HEAD_DIM)
            qs = jnp.pad(qs, ((0, 0), (0, 0), (0, 16 - len(DILATIONS) * n_new), (0, 0)))
            kvs = kv[n_p:n_p + n_s].reshape(n_sb, n_new, 2, N_KV_HEADS, ATT_HEAD_DIM).transpose(2, 0, 3, 1, 4)
            kvs = jnp.pad(kvs, ((0, 0), (0, 0), (0, 0), (0, V7X_LANES - n_new), (0, 0)))
            o_s = attn_sample(qs, kvs[0], kvs[1], cache_k.reshape(n_sb, -1, d_att),
                              cache_v.reshape(n_sb, -1, d_att), n_new)
            o = pad_rows([o_p, o_s.transpose(0, 2, 1, 3).reshape(n_s, d_att)])
            h = matmul(o, b_w_o, b, d_model, 1024, res=h)
        xm, idx, wts = router(h, moe_norm[layer:layer + 1], r_w[layer], r_b[layer])
        plan = moe_plan(idx, tp)
        ys = expert_ffn(moe_dispatch(xm, plan), plan, expert_w_gate, expert_w_up, expert_w_down, layer)
        if layer + 1 < n_a:
            h, xn = moe_combine_norm(ys, plan, wts, h, a_norm[layer + 1:layer + 2], BF16, emit_sum=True)
        elif layer + 1 == n_a:
            h, xkv, xn = moe_combine_norm(ys, plan, wts, h, jnp.stack([kv_norm, b_norm[0]]), BF16, emit_sum=True)
            kv = matmul(xkv, w_kv, None, w_kv.shape[1], 1024)
        elif layer + 1 < depth:
            h, xn = moe_combine_norm(ys, plan, wts, h, b_norm[layer + 1 - n_a:layer + 2 - n_a], BF16,
                                     emit_sum=True)
        else:
            (y_out,) = moe_combine_norm(ys, plan, wts, h, final_norm[None, :], F32, emit_sum=False)

    y_prompt = y_out[:n_p].reshape(n_pb, seq, d_model)
    y_sample = y_out[n_p:n_p + n_s].reshape(n_sb, n_new, d_model)
    keep = min(max(DILATIONS) * (N_DIL_KEYS - 1), seq)
    kv_p = kv[:n_p].reshape(n_pb, seq, 2, N_KV_HEADS, ATT_HEAD_DIM)[:, seq - keep:]
    kv_s = kv[n_p:n_p + n_s].reshape(n_sb, n_new, 2, N_KV_HEADS, ATT_HEAD_DIM)
    ssm_shape = (n_heads, SSM_HEAD_DIM, D_STATE)
    return (y_prompt, y_sample,
            jnp.stack(conv_p), jnp.stack(ssm_p).reshape((n_a, n_pb) + ssm_shape),
            kv_p[:, :, 0], kv_p[:, :, 1],
            jnp.stack(conv_s), jnp.stack(ssm_s).reshape((n_a, n_sb) + ssm_shape),
            kv_s[:, :, 0], kv_s[:, :, 1])
```

```python
import functools
import math

import jax
import jax.numpy as jnp
from jax import lax
from jax.experimental import pallas as pl
from jax.experimental.pallas import tpu as pltpu

F32 = jnp.float32
BF16 = jnp.bfloat16

V7X_VMEM_BYTES = 64 * 1024 * 1024
V7X_LANES = 128
V7X_SUBLANES = 8
VMEM_LIMIT = 56 * 1024 * 1024

RMS_EPS = 1e-6
SSM_HEAD_DIM = 64
N_SSM_GROUPS = 8
D_STATE = 128
CONV_W = 4
SSD_CHUNK = 128
ATT_HEAD_DIM = 128
N_KV_HEADS = 8
DILATIONS = (1, 4, 16)
N_DIL_KEYS = 129
assert all(d & (d - 1) == 0 for d in DILATIONS)
N_EXPERT_GROUPS = 4
EXPERTS_PER_GROUP = 8
N_EXPERTS = N_EXPERT_GROUPS * EXPERTS_PER_GROUP
NEG_INF = -1e30

TOKEN_TILE = 256
MM_ROW_TILE = 768
MOE_ROW_TILE = 256
ATTN_UNROLL = 4


def _cparams(sem):
    return pltpu.CompilerParams(dimension_semantics=sem, vmem_limit_bytes=VMEM_LIMIT)


def _norm_kernel(*refs, n_add, n_out, emit_sum):
    h = refs[0][...]
    for a in refs[1:1 + n_add]:
        h = h + a[...]
    g_ref = refs[1 + n_add]
    outs = refs[2 + n_add:]
    if emit_sum:
        outs[0][...] = h
        outs = outs[1:]
    xn = h * lax.rsqrt(jnp.mean(h * h, axis=-1, keepdims=True) + RMS_EPS)
    for k in range(n_out):
        outs[k][...] = (xn * g_ref[k:k + 1, :]).astype(outs[k].dtype)


def add_norm(h, adds, gains, out_dtype, emit_sum):
    tp, d = h.shape
    n_out = gains.shape[0]
    tm = min(TOKEN_TILE, tp)
    in_specs = [pl.BlockSpec((tm, d), lambda i: (i, 0))]
    args = [h]
    for arr, off in adds:
        in_specs.append(pl.BlockSpec((tm, d), functools.partial(lambda i, o: (i + o, 0), o=off)))
        args.append(arr)
    in_specs.append(pl.BlockSpec((n_out, d), lambda i: (0, 0)))
    args.append(gains)
    out_shape, out_specs = [], []
    if emit_sum:
        out_shape.append(jax.ShapeDtypeStruct((tp, d), F32))
        out_specs.append(pl.BlockSpec((tm, d), lambda i: (i, 0)))
    for _ in range(n_out):
        out_shape.append(jax.ShapeDtypeStruct((tp, d), out_dtype))
        out_specs.append(pl.BlockSpec((tm, d), lambda i: (i, 0)))
    return pl.pallas_call(
        functools.partial(_norm_kernel, n_add=len(adds), n_out=n_out, emit_sum=emit_sum),
        grid=(tp // tm,), in_specs=in_specs, out_specs=out_specs, out_shape=out_shape,
        compiler_params=_cparams(("parallel",)), name="add_norm",
    )(*args)


def _mm_kernel(*refs, has_res, w_is_nk):
    if has_res:
        x_ref, w_ref, r_ref, o_ref, wb_ref = refs
    else:
        x_ref, w_ref, o_ref, wb_ref = refs

    @pl.when(pl.program_id(1) == 0)
    def _():
        wb_ref[...] = w_ref[...].astype(BF16)

    contract = (((1,), (1 if w_is_nk else 0,)), ((), ()))
    acc = lax.dot_general(x_ref[...].astype(BF16), wb_ref[...], contract, preferred_element_type=F32)
    if has_res:
        acc = r_ref[...] + acc
    o_ref[...] = acc


def matmul(x, w, layer, n_cols, tn, res=None, tm=MM_ROW_TILE, w_is_nk=False):
    tp, k = x.shape
    assert tp % tm == 0 and n_cols % tn == 0
    if w_is_nk:
        w_spec = pl.BlockSpec((None, tn, k), lambda j, i: (layer, j, 0))
    elif layer is None:
        w_spec = pl.BlockSpec((k, tn), lambda j, i: (0, j))
    else:
        w_spec = pl.BlockSpec((None, k, tn), lambda j, i: (layer, 0, j))
    in_specs = [pl.BlockSpec((tm, k), lambda j, i: (i, 0)), w_spec]
    args = [x, w]
    if res is not None:
        in_specs.append(pl.BlockSpec((tm, tn), lambda j, i: (i, j)))
        args.append(res)
    return pl.pallas_call(
        functools.partial(_mm_kernel, has_res=res is not None, w_is_nk=w_is_nk),
        grid=(n_cols // tn, tp // tm), in_specs=in_specs,
        out_specs=pl.BlockSpec((tm, tn), lambda j, i: (i, j)),
        out_shape=jax.ShapeDtypeStruct((tp, n_cols), F32),
        scratch_shapes=[pltpu.VMEM((tn, k) if w_is_nk else (k, tn), BF16)],
        compiler_params=_cparams(("arbitrary", "arbitrary")), name="proj",
    )(*args)


def _mm_f32_kernel(*refs, has_res, w_is_nk):
    if has_res:
        x_ref, w_ref, r_ref, o_ref = refs
    else:
        x_ref, w_ref, o_ref = refs
    contract = (((1,), (1 if w_is_nk else 0,)), ((), ()))
    acc = lax.dot_general(x_ref[...], w_ref[...], contract, preferred_element_type=F32,
                          precision=lax.Precision.HIGHEST)
    if has_res:
        acc = r_ref[...] + acc
    o_ref[...] = acc


def matmul_f32(x, w, layer, n_cols, tn, res=None, w_is_nk=False):
    m, k = x.shape
    if w_is_nk:
        w_spec = pl.BlockSpec((None, tn, k), lambda j: (layer, j, 0))
    else:
        w_spec = pl.BlockSpec((None, k, tn), lambda j: (layer, 0, j))
    in_specs = [pl.BlockSpec((m, k), lambda j: (0, 0)), w_spec]
    args = [x, w]
    if res is not None:
        in_specs.append(pl.BlockSpec((m, tn), lambda j: (0, j)))
        args.append(res)
    return pl.pallas_call(
        functools.partial(_mm_f32_kernel, has_res=res is not None, w_is_nk=w_is_nk),
        grid=(n_cols // tn,), in_specs=in_specs, out_specs=pl.BlockSpec((m, tn), lambda j: (0, j)),
        out_shape=jax.ShapeDtypeStruct((m, n_cols), F32),
        compiler_params=_cparams(("parallel",)), name="proj_f32",
    )(*args)


def _silu(x):
    return x * (0.5 * jnp.tanh(0.5 * x) + 0.5)


def _split2(v):
    hi = v.astype(BF16)
    lo = (v - hi.astype(F32)).astype(BF16)
    return hi, lo


def _ssd_kernel(z_ref, x_ref, b_ref, c_ref, dt_ref, conv0_ref, ssm0_ref,
                cw_ref, cb_ref, dtb_ref, alog_ref, dsk_ref, gn_ref, e_ref,
                y_ref, convo_ref, ssmo_ref,
                xpad, xs_s, xlo_s, xhi_s, xw_s, bm_s, bmt_s, cm_s, y_s, ea_s, state_s,
                *, n_valid, has_init):
    L = SSD_CHUNK
    mx = xlo_s.dtype
    prec = lax.Precision.HIGHEST if mx == F32 else None
    d_inner = xs_s.shape[1]
    gn = bm_s.shape[1]
    n_heads = d_inner // SSM_HEAD_DIM
    pair = 2 * SSM_HEAD_DIM
    gcols = d_inner // N_SSM_GROUPS
    c = pl.program_id(1)
    pad0 = V7X_SUBLANES
    keep = CONV_W - 1

    @pl.when(c == 0)
    def _():
        if has_init:
            xpad[pad0 - keep:pad0, :] = conv0_ref[...]
            for g in range(N_SSM_GROUPS):
                state_s[:, g * gcols:(g + 1) * gcols] = ssm0_ref[g * gcols:(g + 1) * gcols, :].T
        else:
            xpad[0:pad0, :] = jnp.zeros((pad0, xpad.shape[1]), F32)
            state_s[...] = jnp.zeros(state_s.shape, F32)

    xpad[pad0:pad0 + L, 0:d_inner] = x_ref[...]
    xpad[pad0:pad0 + L, d_inner:d_inner + gn] = b_ref[...]
    xpad[pad0:pad0 + L, d_inner + gn:d_inner + 2 * gn] = c_ref[...]
    lane = lax.broadcasted_iota(jnp.int32, (L, 512), 1)
    even_head = (lane // SSM_HEAD_DIM) % 2 == 0
    strip = 512
    for s in range(xpad.shape[1] // strip):
        cs = slice(s * strip, (s + 1) * strip)
        acc = jnp.broadcast_to(cb_ref[:, cs], (L, strip))
        for k in range(CONV_W):
            acc = acc + cw_ref[k:k + 1, cs] * xpad[pad0 - keep + k:pad0 - keep + k + L, cs]
        v = _silu(acc)
        if s * strip < d_inner:
            xs_s[:, cs] = v
            vb = v.astype(mx)
            xlo_s[:, cs] = jnp.where(even_head, vb, jnp.zeros_like(vb))
            xhi_s[:, cs] = jnp.where(even_head, jnp.zeros_like(vb), vb)
        elif s * strip < d_inner + gn:
            o = s * strip - d_inner
            bm_s[:, o:o + strip] = v.astype(mx)
            for q in range(strip // D_STATE):
                gi = o // D_STATE + q
                bmt_s[:, gi * L:(gi + 1) * L] = v[:, q * D_STATE:(q + 1) * D_STATE].T.astype(mx)
        else:
            o = s * strip - d_inner - gn
            cm_s[:, o:o + strip] = v.astype(mx)

    tail = xpad[pad0 + n_valid - keep:pad0 + n_valid, :]
    xpad[pad0 - keep:pad0, :] = tail

    @pl.when(c == pl.num_programs(1) - 1)
    def _():
        convo_ref[...] = tail

    row = lax.broadcasted_iota(jnp.int32, (L, V7X_LANES), 0)
    dtv = dt_ref[...] + dtb_ref[...]
    dt = jnp.maximum(dtv, 0.0) + jnp.log(1.0 + jnp.exp(-jnp.abs(dtv)))
    dt = jnp.where(row < n_valid, dt, 0.0)
    a = -jnp.exp(alog_ref[...])
    adt = dt * a
    ii = lax.broadcasted_iota(jnp.int32, (L, L), 0)
    jj = lax.broadcasted_iota(jnp.int32, (L, L), 1)
    causal = ii >= jj
    tri = jnp.where(causal, 1.0, 0.0).astype(F32)
    a_cs = jnp.dot(tri, adt, preferred_element_type=F32, precision=lax.Precision.HIGHEST)
    a_cs_t = a_cs.T
    dt_t = dt.T
    ea = jnp.exp(a_cs)
    dte = dt * jnp.exp(a_cs[L - 1:L, :] - a_cs)
    both = jnp.concatenate([ea, dte], axis=0)
    hi, lo = _split2(both)
    for s in range(d_inner // 512):
        cs = slice(s * 512, (s + 1) * 512)
        if mx == F32:
            ex = jnp.dot(both, e_ref[:, cs].astype(F32), preferred_element_type=F32, precision=prec)
        else:
            ex = (jnp.dot(hi, e_ref[:, cs], preferred_element_type=F32)
                  + jnp.dot(lo, e_ref[:, cs], preferred_element_type=F32))
        ea_s[:, cs] = ex[0:L]
        xw_s[:, cs] = (xs_s[:, cs] * ex[L:2 * L]).astype(mx)

    for g in range(N_SSM_GROUPS):
        gs = slice(g * gcols, (g + 1) * gcols)
        ns = slice(g * D_STATE, (g + 1) * D_STATE)
        cg = cm_s[:, ns]
        cb = lax.dot_general(cg, bm_s[:, ns], (((1,), (1,)), ((), ())), preferred_element_type=F32,
                             precision=prec)
        st = state_s[:, gs]
        y_off = jnp.dot(cg, st.astype(mx), preferred_element_type=F32, precision=prec)
        for q in range(gcols // pair):
            col = slice(g * gcols + q * pair, g * gcols + (q + 1) * pair)
            acc = None
            for hh, xsrc in ((0, xlo_s), (1, xhi_s)):
                h = (g * gcols + q * pair) // SSM_HEAD_DIM + hh
                seg = a_cs[:, h:h + 1] - a_cs_t[h:h + 1, :]
                dec = jnp.exp(jnp.where(causal, seg, NEG_INF))
                m = (cb * dec * dt_t[h:h + 1, :]).astype(mx)
                part = jnp.dot(m, xsrc[:, col], preferred_element_type=F32, precision=prec)
                acc = part if acc is None else acc + part
            lc = slice(q * pair, (q + 1) * pair)
            y_s[:, col] = acc + y_off[:, lc] * ea_s[:, col] + dsk_ref[:, col] * xs_s[:, col]
        new = st * ea_s[L - 1:L, gs] + jnp.dot(bmt_s[:, g * L:(g + 1) * L], xw_s[:, gs],
                                               preferred_element_type=F32, precision=prec)
        state_s[:, gs] = new

    for g in range(N_SSM_GROUPS):
        gs = slice(g * gcols, (g + 1) * gcols)
        v = y_s[:, gs] * _silu(z_ref[:, gs])
        v = v * lax.rsqrt(jnp.mean(v * v, axis=-1, keepdims=True) + RMS_EPS)
        y_ref[:, gs] = (v * gn_ref[:, gs]).astype(y_ref.dtype)

    @pl.when(c == pl.num_programs(1) - 1)
    def _():
        for g in range(N_SSM_GROUPS):
            ssmo_ref[g * gcols:(g + 1) * gcols, :] = state_s[:, g * gcols:(g + 1) * gcols].T


def ssd_mixer(zxbc, dt_raw, row0, n_seq, n_chunks, n_valid, conv0, ssm0, prm, layer, d_inner, gn,
              precise=False):
    L = SSD_CHUNK
    assert row0 % L == 0
    rb = row0 // L
    mx = F32 if precise else BF16
    conv_dim = d_inner + 2 * gn
    has_init = conv0 is not None
    if not has_init:
        conv0 = jnp.zeros((n_seq, CONV_W - 1, conv_dim), F32)
        ssm0 = jnp.zeros((n_seq, 8, D_STATE), F32)
    n_heads = d_inner // SSM_HEAD_DIM
    rows = n_seq * n_chunks * L

    def rmap(b, c):
        return rb + b * n_chunks + c

    zb = d_inner // gn
    in_specs = [
        pl.BlockSpec((L, d_inner), lambda b, c: (rmap(b, c), 0)),
        pl.BlockSpec((L, d_inner), lambda b, c: (rmap(b, c), 1)),
        pl.BlockSpec((L, gn), lambda b, c: (rmap(b, c), 2 * zb)),
        pl.BlockSpec((L, gn), lambda b, c: (rmap(b, c), 2 * zb + 1)),
        pl.BlockSpec((L, V7X_LANES), lambda b, c: (rmap(b, c), 0)),
        pl.BlockSpec((None, CONV_W - 1, conv_dim), lambda b, c: (b, 0, 0)),
        pl.BlockSpec((None,) + ssm0.shape[1:], lambda b, c: (b, 0, 0)),
        pl.BlockSpec((None, CONV_W, conv_dim), lambda b, c: (layer, 0, 0)),
        pl.BlockSpec((None, 1, conv_dim), lambda b, c: (layer, 0, 0)),
        pl.BlockSpec((None, 1, V7X_LANES), lambda b, c: (layer, 0, 0)),
        pl.BlockSpec((None, 1, V7X_LANES), lambda b, c: (layer, 0, 0)),
        pl.BlockSpec((None, 1, d_inner), lambda b, c: (layer, 0, 0)),
        pl.BlockSpec((None, 1, d_inner), lambda b, c: (layer, 0, 0)),
        pl.BlockSpec((V7X_LANES, d_inner), lambda b, c: (0, 0)),
    ]
    out_shape = [
        jax.ShapeDtypeStruct((rows, d_inner), mx),
        jax.ShapeDtypeStruct((n_seq, CONV_W - 1, conv_dim), F32),
        jax.ShapeDtypeStruct((n_seq, n_heads * SSM_HEAD_DIM, D_STATE), F32),
    ]
    out_specs = [
        pl.BlockSpec((L, d_inner), lambda b, c: (b * n_chunks + c, 0)),
        pl.BlockSpec((None, CONV_W - 1, conv_dim), lambda b, c: (b, 0, 0)),
        pl.BlockSpec((None, n_heads * SSM_HEAD_DIM, D_STATE), lambda b, c: (b, 0, 0)),
    ]
    scratch = [
        pltpu.VMEM((L + V7X_SUBLANES, conv_dim), F32),
        pltpu.VMEM((L, d_inner), F32),
        pltpu.VMEM((L, d_inner), mx),
        pltpu.VMEM((L, d_inner), mx),
        pltpu.VMEM((L, d_inner), mx),
        pltpu.VMEM((L, gn), mx),
        pltpu.VMEM((D_STATE, N_SSM_GROUPS * L), mx),
        pltpu.VMEM((L, gn), mx),
        pltpu.VMEM((L, d_inner), F32),
        pltpu.VMEM((L, d_inner), F32),
        pltpu.VMEM((D_STATE, d_inner), F32),
    ]
    return pl.pallas_call(
        functools.partial(_ssd_kernel, n_valid=n_valid, has_init=has_init),
        grid=(n_seq, n_chunks), in_specs=in_specs, out_specs=out_specs, out_shape=out_shape,
        scratch_shapes=scratch,
        compiler_params=_cparams(("arbitrary", "arbitrary")), name="ssd_mixer",
    )(zxbc, zxbc, zxbc, zxbc, dt_raw, conv0, ssm0,
      prm["conv_w"], prm["conv_b"], prm["dt_bias"], prm["a_log"], prm["d_skip"], prm["gate_norm"],
      prm["expand"])


def _attn_prompt_kernel(q0_ref, q1_ref, q2_ref, k_ref, v_ref, o_ref, *scratch, seq):
    og_s, lse_s = scratch[:len(DILATIONS)], scratch[len(DILATIONS):]
    QB = 128
    scale = ATT_HEAD_DIM ** -0.5
    qq = lax.broadcasted_iota(jnp.int32, (QB, QB), 0)
    kk = lax.broadcasted_iota(jnp.int32, (QB, QB), 1)
    cur_ok = kk <= qq
    kq = kk - qq

    def scores(q, k_rows, ok):
        s = lax.dot_general(q, k_rows.astype(BF16), (((1,), (1,)), ((), ())),
                            preferred_element_type=F32) * scale
        return jnp.where(ok, s, NEG_INF)

    for g, (q_ref, d) in enumerate(zip((q0_ref, q1_ref, q2_ref), DILATIONS)):
        nb_r = seq // d // QB

        def body(idx, carry, g=g, q_ref=q_ref, d=d, nb_r=nb_r):
            r = idx // nb_r
            mb = idx % nb_r
            start = r + d * QB * mb
            pstart = jnp.maximum(start - d * QB, r)
            prev_min = jnp.where(mb > 0, 0, QB)
            rows = pl.ds(start, QB, stride=d) if d > 1 else pl.ds(start, QB)
            prows = pl.ds(pstart, QB, stride=d) if d > 1 else pl.ds(pstart, QB)
            q = q_ref[rows, :].astype(BF16)
            s_c = scores(q, k_ref[rows, :], cur_ok)
            s_p = scores(q, k_ref[prows, :], kq >= prev_min)
            m = jnp.maximum(jnp.max(s_c, axis=-1, keepdims=True), jnp.max(s_p, axis=-1, keepdims=True))
            p_c = jnp.exp(s_c - m)
            p_p = jnp.exp(s_p - m)
            den = jnp.sum(p_c, axis=-1, keepdims=True) + jnp.sum(p_p, axis=-1, keepdims=True)
            pv = (jnp.dot(p_c.astype(BF16), v_ref[rows, :].astype(BF16), preferred_element_type=F32)
                  + jnp.dot(p_p.astype(BF16), v_ref[prows, :].astype(BF16), preferred_element_type=F32))
            og_s[g][rows, :] = pv / den
            lse_s[g][rows, :] = jnp.broadcast_to(m + jnp.log(den), (QB, QB))
            return carry

        lax.fori_loop(0, seq // QB, body, 0, unroll=ATTN_UNROLL)

    def merge(i, carry):
        rows = pl.ds(pl.multiple_of(i * QB, QB), QB)
        lse = [lse_s[g][rows, :] for g in range(len(DILATIONS))]
        top = functools.reduce(jnp.maximum, lse)
        w = [jnp.exp(x - top) for x in lse]
        num = functools.reduce(lambda a, b: a + b, [w[g] * og_s[g][rows, :] for g in range(len(DILATIONS))])
        o_ref[rows, :] = num / functools.reduce(lambda a, b: a + b, w)
        return carry

    lax.fori_loop(0, seq // QB, merge, 0, unroll=ATTN_UNROLL)


def attn_prompt(q, kv, n_seq, seq):
    hd = ATT_HEAD_DIM
    nh = N_KV_HEADS

    def qspec(g):
        return pl.BlockSpec((seq, hd), lambda b, h: (b, g * nh + h))

    return pl.pallas_call(
        functools.partial(_attn_prompt_kernel, seq=seq),
        grid=(n_seq, nh),
        in_specs=[qspec(0), qspec(1), qspec(2),
                  pl.BlockSpec((seq, hd), lambda b, h: (b, h)),
                  pl.BlockSpec((seq, hd), lambda b, h: (b, nh + h))],
        out_specs=pl.BlockSpec((seq, hd), lambda b, h: (b, h)),
        out_shape=jax.ShapeDtypeStruct((n_seq * seq, nh * hd), F32),
        scratch_shapes=[pltpu.VMEM((seq, hd), F32)] * (2 * len(DILATIONS)),
        compiler_params=_cparams(("parallel", "parallel")), name="attn_prompt",
    )(q, q, q, kv, kv)


def _attn_sample_kernel(q_ref, kn_ref, vn_ref, kc_ref, vc_ref, o_ref, *, n_new, past):
    QR = q_ref.shape[0]
    scale = ATT_HEAD_DIM ** -0.5
    q = q_ref[...].astype(BF16)

    def dist_ok(dist, dil):
        return jnp.logical_and(jnp.logical_and(dist >= 0, dist <= dil * (N_DIL_KEYS - 1)),
                               jnp.bitwise_and(dist, dil - 1) == 0)

    def mask(n_keys, key0):
        rr = lax.broadcasted_iota(jnp.int32, (QR, n_keys), 0)
        kpos = lax.broadcasted_iota(jnp.int32, (QR, n_keys), 1) + key0
        g = rr // n_new
        qpos = past + rr % n_new
        dil = jnp.where(g == 0, DILATIONS[0], jnp.where(g == 1, DILATIONS[1], DILATIONS[2]))
        ok = dist_ok(qpos - kpos, dil)
        return jnp.logical_and(ok, g < len(DILATIONS))

    ok_c = mask(past, 0)
    ok_n = jnp.logical_and(mask(kn_ref.shape[0], past),
                           lax.broadcasted_iota(jnp.int32, (QR, kn_ref.shape[0]), 1) < n_new)
    head_rows = pl.ds(pl.program_id(1), past, stride=N_KV_HEADS)
    s_c = lax.dot_general(q, kc_ref[head_rows, :].astype(BF16), (((1,), (1,)), ((), ())),
                          preferred_element_type=F32) * scale
    s_n = lax.dot_general(q, kn_ref[...].astype(BF16), (((1,), (1,)), ((), ())),
                          preferred_element_type=F32) * scale
    s_c = jnp.where(ok_c, s_c, NEG_INF)
    s_n = jnp.where(ok_n, s_n, NEG_INF)
    m = jnp.maximum(jnp.max(s_c, axis=-1, keepdims=True), jnp.max(s_n, axis=-1, keepdims=True))
    p_c = jnp.exp(s_c - m)
    p_n = jnp.exp(s_n - m)
    l = jnp.sum(p_c, axis=-1, keepdims=True) + jnp.sum(p_n, axis=-1, keepdims=True)
    pv = (jnp.dot(p_c.astype(BF16), vc_ref[head_rows, :].astype(BF16), preferred_element_type=F32)
          + jnp.dot(p_n.astype(BF16), vn_ref[...].astype(BF16), preferred_element_type=F32))
    ms = [m[g * n_new:(g + 1) * n_new] for g in range(len(DILATIONS))]
    m_all = functools.reduce(jnp.maximum, ms)
    num = jnp.zeros((n_new, ATT_HEAD_DIM), F32)
    den = jnp.zeros((n_new, 1), F32)
    for g in range(len(DILATIONS)):
        w = jnp.exp(ms[g] - m_all)
        num = num + w * pv[g * n_new:(g + 1) * n_new]
        den = den + w * l[g * n_new:(g + 1) * n_new]
    o_ref[...] = num / den


def attn_sample(q_s, k_new, v_new, cache_k, cache_v, n_new):
    nb, nh, qr, hd = q_s.shape
    past = cache_k.shape[1] // nh
    slab = pl.BlockSpec((None, past * nh, hd), lambda b, h: (b, 0, 0))
    return pl.pallas_call(
        functools.partial(_attn_sample_kernel, n_new=n_new, past=past),
        grid=(nb, nh),
        in_specs=[pl.BlockSpec((None, None, qr, hd), lambda b, h: (b, h, 0, 0)),
                  pl.BlockSpec((None, None, k_new.shape[2], hd), lambda b, h: (b, h, 0, 0)),
                  pl.BlockSpec((None, None, k_new.shape[2], hd), lambda b, h: (b, h, 0, 0)),
                  slab, slab],
        out_specs=pl.BlockSpec((None, None, n_new, hd), lambda b, h: (b, h, 0, 0)),
        out_shape=jax.ShapeDtypeStruct((nb, nh, n_new, hd), F32),
        compiler_params=_cparams(("arbitrary", "arbitrary")), name="attn_sample",
    )(q_s, k_new, v_new, cache_k, cache_v)


def _router_kernel(h_ref, g_ref, w_ref, b_ref, xn_ref, idx_ref, wt_ref):
    h = h_ref[...]
    xn = h * lax.rsqrt(jnp.mean(h * h, axis=-1, keepdims=True) + RMS_EPS) * g_ref[...]
    xn_ref[...] = xn.astype(xn_ref.dtype)
    logits = jnp.dot(xn, w_ref[...], preferred_element_type=F32,
                     precision=lax.Precision.HIGHEST) + b_ref[...]
    tm = logits.shape[0]
    lane = lax.broadcasted_iota(jnp.int32, logits.shape, 1)
    big = jnp.int32(V7X_LANES)
    is_g = lane < N_EXPERT_GROUPS
    gl = jnp.where(is_g, logits, NEG_INF)
    gmax = jnp.max(gl, axis=-1, keepdims=True)
    g_idx = jnp.min(jnp.where(jnp.logical_and(is_g, gl == gmax), lane, big), axis=-1, keepdims=True)
    g_w = 1.0 / jnp.sum(jnp.where(is_g, jnp.exp(gl - gmax), 0.0), axis=-1, keepdims=True)
    lo = N_EXPERT_GROUPS + g_idx * EXPERTS_PER_GROUP
    in_grp = jnp.logical_and(lane >= lo, lane < lo + EXPERTS_PER_GROUP)
    el = jnp.where(in_grp, logits, NEG_INF)
    v1 = jnp.max(el, axis=-1, keepdims=True)
    i1 = jnp.min(jnp.where(jnp.logical_and(in_grp, el == v1), lane, big), axis=-1, keepdims=True)
    el2 = jnp.where(lane == i1, NEG_INF, el)
    v2 = jnp.max(el2, axis=-1, keepdims=True)
    i2 = jnp.min(jnp.where(jnp.logical_and(in_grp, jnp.logical_and(el2 == v2, lane != i1)), lane, big),
                 axis=-1, keepdims=True)
    e2 = jnp.exp(v2 - v1)
    w1 = g_w / (1.0 + e2)
    w2 = g_w * e2 / (1.0 + e2)
    col = lax.broadcasted_iota(jnp.int32, (tm, 2), 1)
    idx_ref[...] = jnp.where(col == 0, i1, i2) - N_EXPERT_GROUPS
    wt_ref[...] = jnp.where(col == 0, w1, w2)


def router(h, gain, w_cat, b_cat):
    tp, d = h.shape
    tm = TOKEN_TILE
    return pl.pallas_call(
        _router_kernel, grid=(tp // tm,),
        in_specs=[pl.BlockSpec((tm, d), lambda i: (i, 0)),
                  pl.BlockSpec((1, d), lambda i: (0, 0)),
                  pl.BlockSpec((d, V7X_LANES), lambda i: (0, 0)),
                  pl.BlockSpec((1, V7X_LANES), lambda i: (0, 0))],
        out_specs=[pl.BlockSpec((tm, d), lambda i: (i, 0)),
                   pl.BlockSpec((tm, 2), lambda i: (i, 0)),
                   pl.BlockSpec((tm, 2), lambda i: (i, 0))],
        out_shape=[jax.ShapeDtypeStruct((tp, d), BF16),
                   jax.ShapeDtypeStruct((tp, 2), jnp.int32),
                   jax.ShapeDtypeStruct((tp, 2), F32)],
        compiler_params=_cparams(("parallel",)), name="router",
    )(h, gain, w_cat, b_cat)


RUN_ALIGN = 16
RUN_CHUNK = 64
LOCAL_ROWS = 2 * TOKEN_TILE + N_EXPERTS * RUN_ALIGN


def _for_pieces(n, chunk, fn):
    n_big = n // chunk
    lax.fori_loop(0, n_big, lambda j, c: (fn(j * chunk, chunk), c)[1], 0)
    done = n_big * chunk
    size = chunk // 2
    while size >= RUN_ALIGN:
        hit = (n & size) != 0
        pl.when(hit)(functools.partial(fn, done, size))
        done = done + jnp.where(hit, size, 0)
        size //= 2


def _for_run_pieces(plen_ref, loff_ref, gdst_ref, blk, fn):
    def per_run(e, carry):
        k = blk * N_EXPERTS + e
        lo, gd = loff_ref[k], gdst_ref[k]
        _for_pieces(plen_ref[k], RUN_CHUNK,
                    lambda off, size: fn(pl.multiple_of(lo + off, RUN_ALIGN),
                                         pl.multiple_of(gd + off, RUN_ALIGN), size))
        return carry
    lax.fori_loop(0, N_EXPERTS, per_run, 0)


def _dispatch_kernel(plen_ref, loff_ref, gdst_ref, fill0_ref, filln_ref,
                     x_ref, src_ref, xs_hbm, loc, zeros, sem, zsem):
    b = pl.program_id(0)
    nb = pl.num_programs(0)
    slot = b % 2

    def copy(s, l, g, size):
        return pltpu.make_async_copy(loc.at[s, pl.ds(l, size)], xs_hbm.at[pl.ds(g, size)], sem.at[s])

    def wait_block(blk, s):
        _for_run_pieces(plen_ref, loff_ref, gdst_ref, blk, lambda l, g, size: copy(s, l, g, size).wait())

    @pl.when(b >= 2)
    def _():
        wait_block(b - 2, slot)

    tok = lax.broadcasted_iota(jnp.int32, (LOCAL_ROWS, TOKEN_TILE), 1)
    pick = jnp.where(jnp.right_shift(src_ref[...], 1) == tok, 1.0, 0.0).astype(BF16)
    loc[slot] = jnp.dot(pick, x_ref[...], preferred_element_type=F32).astype(BF16)
    _for_run_pieces(plen_ref, loff_ref, gdst_ref, b, lambda l, g, size: copy(slot, l, g, size).start())

    @pl.when(b == nb - 1)
    def _():
        @pl.when(b >= 1)
        def _():
            wait_block(b - 1, 1 - slot)
        wait_block(b, slot)
        zeros[...] = jnp.zeros(zeros.shape, zeros.dtype)

        def fill(op):
            def per_expert(e, carry):
                f0 = fill0_ref[e]
                _for_pieces(filln_ref[e], MOE_ROW_TILE // 2, lambda off, size: op(pltpu.make_async_copy(
                    zeros.at[pl.ds(0, size)], xs_hbm.at[pl.ds(pl.multiple_of(f0 + off, RUN_ALIGN), size)], zsem)))
                return carry
            lax.fori_loop(0, fill0_ref.shape[0], per_expert, 0)

        fill(lambda c: c.start())
        fill(lambda c: c.wait())


def moe_dispatch(xn, plan):
    tp, d = xn.shape
    nb = tp // TOKEN_TILE
    grid_spec = pltpu.PrefetchScalarGridSpec(
        num_scalar_prefetch=5, grid=(nb,),
        in_specs=[pl.BlockSpec((TOKEN_TILE, d), lambda b, *_: (b, 0)),
                  pl.BlockSpec((None, LOCAL_ROWS, 1), lambda b, *_: (b, 0, 0))],
        out_specs=pl.BlockSpec(memory_space=pl.ANY),
        scratch_shapes=[pltpu.VMEM((2, LOCAL_ROWS, d), BF16), pltpu.VMEM((MOE_ROW_TILE // 2, d), BF16),
                        pltpu.SemaphoreType.DMA((2,)), pltpu.SemaphoreType.DMA],
    )
    return pl.pallas_call(
        _dispatch_kernel, grid_spec=grid_spec,
        out_shape=jax.ShapeDtypeStruct((plan["n_rows"], d), BF16),
        compiler_params=_cparams(("arbitrary",)), name="moe_dispatch",
    )(plan["plen"], plan["loff"], plan["gdst"], plan["fill0"], plan["filln"], xn, plan["src_col"])


def _expert_kernel(te_ref, nu_ref, first_ref, nxt_ref, par_ref,
                   xs_ref, wg_hbm, wu_hbm, wd_hbm, ys_ref,
                   wgf, wuf, wdf, wgb, wub, wdb, wsem, *, layer):
    i = pl.program_id(0)
    used = i < nu_ref[0]

    def weight_copies(e, s):
        return (pltpu.make_async_copy(wg_hbm.at[layer, e], wgf.at[s], wsem.at[s]),
                pltpu.make_async_copy(wu_hbm.at[layer, e], wuf.at[s], wsem.at[s]),
                pltpu.make_async_copy(wd_hbm.at[layer, e], wdf.at[s], wsem.at[s]))

    @pl.when(used)
    def _():
        @pl.when(first_ref[i] == 1)
        def _():
            s = par_ref[i]

            @pl.when(i == 0)
            def _():
                for c in weight_copies(te_ref[0], s):
                    c.start()

            for c in weight_copies(te_ref[i], s):
                c.wait()

            @pl.when(nxt_ref[i] >= 0)
            def _():
                for c in weight_copies(nxt_ref[i], 1 - s):
                    c.start()

            wgb[...] = wgf[s].astype(BF16)
            wub[...] = wuf[s].astype(BF16)
            wdb[...] = wdf[s].astype(BF16)

        x = xs_ref[...]
        hg = jnp.dot(x, wgb[...], preferred_element_type=F32)
        hu = jnp.dot(x, wub[...], preferred_element_type=F32)
        act = (_silu(hg) * hu).astype(BF16)
        ys_ref[...] = jnp.dot(act, wdb[...], preferred_element_type=F32).astype(ys_ref.dtype)

    @pl.when(jnp.logical_not(used))
    def _():
        ys_ref[...] = jnp.zeros(ys_ref.shape, ys_ref.dtype)


def expert_ffn(xs, plan, w_gate, w_up, w_down, layer):
    n_rows, d = xs.shape
    tm = MOE_ROW_TILE
    de = w_gate.shape[-1]

    grid_spec = pltpu.PrefetchScalarGridSpec(
        num_scalar_prefetch=5, grid=(n_rows // tm,),
        in_specs=[pl.BlockSpec((tm, d), lambda i, te, nu, *_: (jnp.minimum(i, nu[0] - 1), 0)),
                  pl.BlockSpec(memory_space=pl.ANY), pl.BlockSpec(memory_space=pl.ANY),
                  pl.BlockSpec(memory_space=pl.ANY)],
        out_specs=pl.BlockSpec((tm, d), lambda i, *_: (i, 0)),
        scratch_shapes=[pltpu.VMEM((2, d, de), F32), pltpu.VMEM((2, d, de), F32), pltpu.VMEM((2, de, d), F32),
                        pltpu.VMEM((d, de), BF16), pltpu.VMEM((d, de), BF16), pltpu.VMEM((de, d), BF16),
                        pltpu.SemaphoreType.DMA((2,))],
    )
    return pl.pallas_call(
        functools.partial(_expert_kernel, layer=layer), grid_spec=grid_spec,
        out_shape=jax.ShapeDtypeStruct((n_rows, d), BF16),
        compiler_params=_cparams(("arbitrary",)), name="expert_ffn",
    )(plan["tile_expert"], plan["n_used"], plan["tile_first"], plan["tile_next"], plan["tile_slot"],
      xs, w_gate, w_up, w_down)


def _combine_kernel(plen_ref, loff_ref, gdst_ref,
                    ys_hbm, src_ref, wt_ref, h_ref, g_ref, *rest, n_out, emit_sum, head_blocks):
    outs, (buf, sem) = rest[:-2], rest[-2:]
    b = pl.program_id(0)
    nb = pl.num_programs(0)
    slot = b % 2

    def copy(s, l, g, size):
        return pltpu.make_async_copy(ys_hbm.at[pl.ds(g, size)], buf.at[s, pl.ds(l, size)], sem.at[s])

    def fetch(blk, s, op):
        _for_run_pieces(plen_ref, loff_ref, gdst_ref, blk, lambda l, g, size: op(copy(s, l, g, size)))

    @pl.when(b == 0)
    def _():
        buf[...] = jnp.zeros(buf.shape, buf.dtype)
        fetch(0, 0, lambda c: c.start())

    @pl.when(b + 1 < nb)
    def _():
        fetch(b + 1, 1 - slot, lambda c: c.start())

    fetch(b, slot, lambda c: c.wait())
    y = buf[slot]
    tok2 = 2 * lax.broadcasted_iota(jnp.int32, (TOKEN_TILE, LOCAL_ROWS), 0)
    src = src_ref[...]
    h = h_ref[...]
    for k in range(2):
        pick = jnp.where(src == tok2 + k, 1.0, 0.0).astype(BF16)
        h = h + wt_ref[:, k:k + 1] * jnp.dot(pick, y, preferred_element_type=F32)
    if emit_sum:
        outs[0][...] = h
        outs = outs[1:]
    xn = h * lax.rsqrt(jnp.mean(h * h, axis=-1, keepdims=True) + RMS_EPS)
    if head_blocks is not None:
        val = xn * g_ref[0:1, :]

        @pl.when(b < head_blocks)
        def _():
            outs[0][...] = val

        @pl.when(b >= head_blocks)
        def _():
            outs[1][...] = val
        return
    for k in range(n_out):
        outs[k][...] = (xn * g_ref[k:k + 1, :]).astype(outs[k].dtype)


def moe_combine_norm(ys, plan, wts, h, gains, out_dtype, emit_sum, head_blocks=None):
    tp, d = h.shape
    nb = tp // TOKEN_TILE
    n_out = gains.shape[0]
    tok_spec = pl.BlockSpec((TOKEN_TILE, d), lambda b, *_: (b, 0))
    n_res = n_out + (1 if emit_sum else 0)
    if head_blocks is not None:
        assert n_out == 1 and not emit_sum and head_blocks == nb - 1
        out_specs = [pl.BlockSpec((TOKEN_TILE, d), lambda b, *_: (jnp.minimum(b, head_blocks - 1), 0)),
                     pl.BlockSpec((TOKEN_TILE, d), lambda b, *_: (0, 0))]
        out_shape = [jax.ShapeDtypeStruct((head_blocks * TOKEN_TILE, d), out_dtype),
                     jax.ShapeDtypeStruct((TOKEN_TILE, d), out_dtype)]
    else:
        out_specs = [tok_spec] * n_res
        out_shape = ([jax.ShapeDtypeStruct((tp, d), F32)] if emit_sum else []) + [
            jax.ShapeDtypeStruct((tp, d), out_dtype)] * n_out
    grid_spec = pltpu.PrefetchScalarGridSpec(
        num_scalar_prefetch=3, grid=(nb,),
        in_specs=[pl.BlockSpec(memory_space=pl.ANY),
                  pl.BlockSpec((None, 1, LOCAL_ROWS), lambda b, *_: (b, 0, 0)),
                  pl.BlockSpec((TOKEN_TILE, 2), lambda b, *_: (b, 0)),
                  tok_spec,
                  pl.BlockSpec((n_out, d), lambda b, *_: (0, 0))],
        out_specs=out_specs,
        scratch_shapes=[pltpu.VMEM((2, LOCAL_ROWS, d), BF16), pltpu.SemaphoreType.DMA((2,))],
    )
    return pl.pallas_call(
        functools.partial(_combine_kernel, n_out=n_out, emit_sum=emit_sum, head_blocks=head_blocks),
        grid_spec=grid_spec,
        out_shape=out_shape, compiler_params=_cparams(("arbitrary",)), name="moe_combine_norm",
    )(plan["plen"], plan["loff"], plan["gdst"], ys, plan["src_row"], wts, h, gains)


def moe_plan(idx, tp):
    i32 = jnp.int32
    tm = MOE_ROW_TILE
    nb = tp // TOKEN_TILE
    n_asg = 2 * TOKEN_TILE
    n_tiles = -(-(2 * tp + nb * N_EXPERTS * (RUN_ALIGN - 1) + N_EXPERTS * (tm - 1)) // tm)
    e_blk = idx.reshape(nb, n_asg)
    onehot = (e_blk[:, :, None] == jnp.arange(N_EXPERTS, dtype=i32)).astype(i32)
    csum = jnp.cumsum(onehot, axis=1)
    rank = jnp.sum(onehot * csum, axis=2) - 1
    plen = (csum[:, -1, :] + RUN_ALIGN - 1) // RUN_ALIGN * RUN_ALIGN
    loff = jnp.cumsum(plen, axis=1) - plen
    lpos = jnp.sum(onehot * loff[:, None, :], axis=2) + rank
    hit = lpos[:, :, None] == jnp.arange(LOCAL_ROWS, dtype=i32)
    src = jnp.sum(jnp.where(hit, jnp.arange(1, n_asg + 1, dtype=i32)[None, :, None], 0), axis=1) - 1
    seg = jnp.sum(plen, axis=0)
    seg_pad = (seg + tm - 1) // tm * tm
    gend = jnp.cumsum(seg_pad)
    gstart = gend - seg_pad
    gdst = gstart[None, :] + jnp.cumsum(plen, axis=0) - plen
    n_used = gend[-1] // tm
    t_row = jnp.minimum(jnp.arange(n_tiles, dtype=i32), n_used - 1) * tm
    tile_expert = jnp.minimum(jnp.sum(gend[None, :] <= t_row[:, None], axis=1), N_EXPERTS - 1)
    ids = jnp.arange(N_EXPERTS, dtype=i32)
    has = seg > 0
    later = jnp.logical_and(ids[None, :] > ids[:, None], has[None, :])
    nxt_e = jnp.min(jnp.where(later, ids[None, :], N_EXPERTS), axis=1)
    nxt_e = jnp.where(nxt_e < N_EXPERTS, nxt_e, -1)
    slot_e = (jnp.cumsum(has.astype(i32)) - 1) % 2
    te_hot = (tile_expert[:, None] == ids[None, :]).astype(i32)
    tile_first = jnp.concatenate([jnp.ones((1,), i32), (tile_expert[1:] != tile_expert[:-1]).astype(i32)])
    plan_tiles = {"tile_first": tile_first, "tile_next": jnp.sum(te_hot * nxt_e[None, :], axis=1),
                  "tile_slot": jnp.sum(te_hot * slot_e[None, :], axis=1)}
    return {
        **{k: v.astype(i32) for k, v in plan_tiles.items()},
        "plen": plen.reshape(-1).astype(i32), "loff": loff.reshape(-1).astype(i32),
        "gdst": gdst.reshape(-1).astype(i32),
        "fill0": jnp.concatenate([gstart + seg, gend[-1:]]).astype(i32),
        "filln": jnp.concatenate([seg_pad - seg, n_tiles * tm - gend[-1:]]).astype(i32),
        "src_col": src[:, :, None].astype(i32), "src_row": src[:, None, :].astype(i32),
        "tile_expert": tile_expert.astype(i32), "n_used": n_used.reshape(1).astype(i32),
        "n_rows": n_tiles * tm,
    }


def kernel(x_prompt, x_sample, state_conv, state_ssm, cache_k, cache_v, a_norm, a_w_in, a_conv_w, a_conv_b, a_dt_bias, a_a_log, a_d_skip, a_gate_norm, a_w_out, kv_norm, w_kv, b_norm, b_w_q, b_w_o, moe_norm, router_group_w, router_group_b, router_expert_w, router_expert_b, expert_w_gate, expert_w_up, expert_w_down, final_norm):
    n_pb, seq, d_model = x_prompt.shape
    n_sb, n_new, _ = x_sample.shape
    n_a = a_w_in.shape[0]
    depth = moe_norm.shape[0]
    d_inner = a_w_out.shape[1]
    conv_dim = a_conv_w.shape[2]
    gn = (conv_dim - d_inner) // 2
    n_heads = d_inner // SSM_HEAD_DIM
    d_att = w_kv.shape[1] // 2
    n_p, n_s = n_pb * seq, n_sb * n_new
    lcm = MM_ROW_TILE * TOKEN_TILE // math.gcd(MM_ROW_TILE, TOKEN_TILE)
    tp = -(-(n_p + n_s) // lcm) * lcm
    L = SSD_CHUNK
    assert n_heads <= V7X_LANES and n_new <= L and n_new >= CONV_W - 1 and seq % (max(DILATIONS) * 128) == 0

    def pad_rows(parts):
        rows = sum(p.shape[0] for p in parts)
        return jnp.concatenate(parts + [jnp.zeros((tp - rows,) + parts[0].shape[1:], parts[0].dtype)], axis=0)

    def lanes(v):
        return jnp.pad(v.astype(F32), ((0, 0), (0, V7X_LANES - n_heads)))[:, None, :]

    head_of_col = jnp.arange(d_inner, dtype=jnp.int32) // SSM_HEAD_DIM
    ssd_prm = {
        "conv_w": a_conv_w, "conv_b": a_conv_b[:, None, :],
        "dt_bias": lanes(a_dt_bias), "a_log": lanes(a_a_log),
        "d_skip": jnp.repeat(a_d_skip.astype(F32), SSM_HEAD_DIM, axis=1)[:, None, :],
        "gate_norm": a_gate_norm[:, None, :],
        "expand": (jnp.arange(V7X_LANES, dtype=jnp.int32)[:, None] == head_of_col[None, :]).astype(BF16),
    }
    w_dt = jnp.pad(a_w_in[:, :, 2 * d_inner + 2 * gn:], ((0, 0), (0, 0), (0, V7X_LANES - n_heads)))
    w_in_nk = jnp.swapaxes(a_w_in, 1, 2)
    r_w = jnp.concatenate([router_group_w, router_expert_w.reshape(depth, d_model, N_EXPERTS)], axis=2)
    r_w = jnp.pad(r_w, ((0, 0), (0, 0), (0, V7X_LANES - r_w.shape[2])))
    r_b = jnp.concatenate([router_group_b, router_expert_b.reshape(depth, N_EXPERTS)], axis=1)
    r_b = jnp.pad(r_b, ((0, 0), (0, V7X_LANES - r_b.shape[1])))[:, None, :]

    h = pad_rows([x_prompt.reshape(n_p, d_model), x_sample.reshape(n_s, d_model)])
    (xn,) = add_norm(h, [], a_norm[0:1], BF16, emit_sum=False)
    conv_p, ssm_p, conv_s, ssm_s = [], [], [], []
    kv = None
    for layer in range(depth):
        if layer < n_a:
            zxbc = matmul(xn, w_in_nk, layer, 2 * d_inner + 2 * gn, 1024, w_is_nk=True)
            dt_raw = matmul(xn, w_dt, layer, V7X_LANES, V7X_LANES)
            y_p, cp, sp = ssd_mixer(zxbc, dt_raw, 0, n_pb, seq // L, L, None, None, ssd_prm, layer, d_inner, gn)
            precise = layer == 0
            if precise:
                x_s = x_sample.reshape(n_s, d_model)
                (xn_s,) = add_norm(x_s, [], a_norm[0:1], F32, emit_sum=False)
                zx_s = matmul_f32(xn_s, w_in_nk, layer, 2 * d_inner + 2 * gn, 1024, w_is_nk=True)
                dt_s = matmul_f32(xn_s, w_dt, layer, V7X_LANES, V7X_LANES)
            else:
                zx_s, dt_s = zxbc[n_p:n_p + n_s], dt_raw[n_p:n_p + n_s]
            zs = jnp.pad(zx_s.reshape(n_sb, n_new, -1), ((0, 0), (0, L - n_new), (0, 0)))
            ds = jnp.pad(dt_s.reshape(n_sb, n_new, -1), ((0, 0), (0, L - n_new), (0, 0)))
            y_s, cs, ss = ssd_mixer(zs.reshape(n_sb * L, -1), ds.reshape(n_sb * L, -1), 0, n_sb, 1, n_new,
                                    state_conv[layer], state_ssm[layer].reshape(n_sb, n_heads * SSM_HEAD_DIM, D_STATE),
                                    ssd_prm, layer, d_inner, gn, precise=precise)
            conv_p.append(cp); ssm_p.append(sp); conv_s.append(cs); ssm_s.append(ss)
            y_s = y_s.reshape(n_sb, L, d_inner)[:, :n_new].reshape(n_s, d_inner)
            y = pad_rows([y_p, y_s.astype(BF16)])
            h = matmul(y, a_w_out, layer, d_model, 512, res=h)
            if precise:
                h_s = matmul_f32(y_s, a_w_out, layer, d_model, 512, res=x_s)
                h = lax.dynamic_update_slice(h, h_s, (n_p, 0))
        else:
            b = layer - n_a
            q = matmul(xn, b_w_q, b, b_w_q.shape[2], 1024)
            o_p = attn_prompt(q, kv, n_pb, seq)
            qs = q[n_p:n_p + n_s].reshape(n_sb, n_new, len(DILATIONS), N_KV_HEADS, ATT_HEAD_DIM)
            qs = qs.transpose(0, 3, 2, 1, 4).reshape(n_sb, N_KV_HEADS, len(DILATIONS) * n_new, ATT_HEAD_DIM)
            qs = jnp.pad(qs, ((0, 0), (0, 0), (0, 16 - len(DILATIONS) * n_new), (0, 0)))
            kvs = kv[n_p:n_p + n_s].reshape(n_sb, n_new, 2, N_KV_HEADS, ATT_HEAD_DIM).transpose(2, 0, 3, 1, 4)
            kvs = jnp.pad(kvs, ((0, 0), (0, 0), (0, 0), (0, V7X_LANES - n_new), (0, 0)))
            o_s = attn_sample(qs, kvs[0], kvs[1], cache_k.reshape(n_sb, -1, ATT_HEAD_DIM),
                              cache_v.reshape(n_sb, -1, ATT_HEAD_DIM), n_new)
            o = pad_rows([o_p, o_s.transpose(0, 2, 1, 3).reshape(n_s, d_att)])
            h = matmul(o, b_w_o, b, d_model, 1024, res=h)
        xm, idx, wts = router(h, moe_norm[layer:layer + 1], r_w[layer], r_b[layer])
        plan = moe_plan(idx, tp)
        ys = expert_ffn(moe_dispatch(xm, plan), plan, expert_w_gate, expert_w_up, expert_w_down, layer)
        if layer + 1 < n_a:
            h, xn = moe_combine_norm(ys, plan, wts, h, a_norm[layer + 1:layer + 2], BF16, emit_sum=True)
        elif layer + 1 == n_a:
            h, xkv, xn = moe_combine_norm(ys, plan, wts, h, jnp.stack([kv_norm, b_norm[0]]), BF16, emit_sum=True)
            kv = matmul(xkv, w_kv, None, w_kv.shape[1], 1024)
        elif layer + 1 < depth:
            h, xn = moe_combine_norm(ys, plan, wts, h, b_norm[layer + 1 - n_a:layer + 2 - n_a], BF16,
                                     emit_sum=True)
        else:
            y_head, y_tail = moe_combine_norm(ys, plan, wts, h, final_norm[None, :], F32, emit_sum=False,
                                              head_blocks=n_p // TOKEN_TILE)

    y_prompt = y_head.reshape(n_pb, seq, d_model)
    y_sample = y_tail[:n_s].reshape(n_sb, n_new, d_model)
    keep = min(max(DILATIONS) * (N_DIL_KEYS - 1), seq)
    kv_p = kv[:n_p].reshape(n_pb, seq, 2, N_KV_HEADS, ATT_HEAD_DIM)[:, seq - keep:]
    kv_s = kv[n_p:n_p + n_s].reshape(n_sb, n_new, 2, N_KV_HEADS, ATT_HEAD_DIM)
    ssm_shape = (n_heads, SSM_HEAD_DIM, D_STATE)
    return (y_prompt, y_sample,
            jnp.stack(conv_p), jnp.stack(ssm_p).reshape((n_a, n_pb) + ssm_shape),
            kv_p[:, :, 0], kv_p[:, :, 1],
            jnp.stack(conv_s), jnp.stack(ssm_s).reshape((n_a, n_sb) + ssm_shape),
            kv_s[:, :, 0], kv_s[:, :, 1])
```

```python
import functools
import math

import jax
import jax.numpy as jnp
from jax import lax
from jax.experimental import pallas as pl
from jax.experimental.pallas import tpu as pltpu

F32 = jnp.float32
BF16 = jnp.bfloat16

V7X_VMEM_BYTES = 64 * 1024 * 1024
V7X_LANES = 128
V7X_SUBLANES = 8
VMEM_LIMIT = 56 * 1024 * 1024

RMS_EPS = 1e-6
SSM_HEAD_DIM = 64
N_SSM_GROUPS = 8
D_STATE = 128
CONV_W = 4
SSD_CHUNK = 128
ATT_HEAD_DIM = 128
N_KV_HEADS = 8
DILATIONS = (1, 4, 16)
N_DIL_KEYS = 129
assert all(d & (d - 1) == 0 for d in DILATIONS)
N_EXPERT_GROUPS = 4
EXPERTS_PER_GROUP = 8
N_EXPERTS = N_EXPERT_GROUPS * EXPERTS_PER_GROUP
NEG_INF = -1e30

TOKEN_TILE = 256
MM_ROW_TILE = 768
MOE_ROW_TILE = 256
ATTN_Q_BLOCK = 256
ATTN_UNROLL = 2


def _cparams(sem):
    return pltpu.CompilerParams(dimension_semantics=sem, vmem_limit_bytes=VMEM_LIMIT)


def _norm_kernel(*refs, n_add, n_out, emit_sum):
    h = refs[0][...]
    for a in refs[1:1 + n_add]:
        h = h + a[...]
    g_ref = refs[1 + n_add]
    outs = refs[2 + n_add:]
    if emit_sum:
        outs[0][...] = h
        outs = outs[1:]
    xn = h * lax.rsqrt(jnp.mean(h * h, axis=-1, keepdims=True) + RMS_EPS)
    for k in range(n_out):
        outs[k][...] = (xn * g_ref[k:k + 1, :]).astype(outs[k].dtype)


def add_norm(h, adds, gains, out_dtype, emit_sum):
    tp, d = h.shape
    n_out = gains.shape[0]
    tm = min(TOKEN_TILE, tp)
    in_specs = [pl.BlockSpec((tm, d), lambda i: (i, 0))]
    args = [h]
    for arr, off in adds:
        in_specs.append(pl.BlockSpec((tm, d), functools.partial(lambda i, o: (i + o, 0), o=off)))
        args.append(arr)
    in_specs.append(pl.BlockSpec((n_out, d), lambda i: (0, 0)))
    args.append(gains)
    out_shape, out_specs = [], []
    if emit_sum:
        out_shape.append(jax.ShapeDtypeStruct((tp, d), F32))
        out_specs.append(pl.BlockSpec((tm, d), lambda i: (i, 0)))
    for _ in range(n_out):
        out_shape.append(jax.ShapeDtypeStruct((tp, d), out_dtype))
        out_specs.append(pl.BlockSpec((tm, d), lambda i: (i, 0)))
    return pl.pallas_call(
        functools.partial(_norm_kernel, n_add=len(adds), n_out=n_out, emit_sum=emit_sum),
        grid=(tp // tm,), in_specs=in_specs, out_specs=out_specs, out_shape=out_shape,
        compiler_params=_cparams(("parallel",)), name="add_norm",
    )(*args)


def _mm_kernel(*refs, has_res, w_is_nk):
    if has_res:
        x_ref, w_ref, r_ref, o_ref, wb_ref = refs
    else:
        x_ref, w_ref, o_ref, wb_ref = refs

    @pl.when(pl.program_id(1) == 0)
    def _():
        wb_ref[...] = w_ref[...].astype(BF16)

    contract = (((1,), (1 if w_is_nk else 0,)), ((), ()))
    acc = lax.dot_general(x_ref[...].astype(BF16), wb_ref[...], contract, preferred_element_type=F32)
    if has_res:
        acc = r_ref[...] + acc
    o_ref[...] = acc


def matmul(x, w, layer, n_cols, tn, res=None, tm=MM_ROW_TILE, w_is_nk=False):
    tp, k = x.shape
    assert tp % tm == 0 and n_cols % tn == 0
    if w_is_nk:
        w_spec = pl.BlockSpec((None, tn, k), lambda j, i: (layer, j, 0))
    elif layer is None:
        w_spec = pl.BlockSpec((k, tn), lambda j, i: (0, j))
    else:
        w_spec = pl.BlockSpec((None, k, tn), lambda j, i: (layer, 0, j))
    in_specs = [pl.BlockSpec((tm, k), lambda j, i: (i, 0)), w_spec]
    args = [x, w]
    if res is not None:
        in_specs.append(pl.BlockSpec((tm, tn), lambda j, i: (i, j)))
        args.append(res)
    return pl.pallas_call(
        functools.partial(_mm_kernel, has_res=res is not None, w_is_nk=w_is_nk),
        grid=(n_cols // tn, tp // tm), in_specs=in_specs,
        out_specs=pl.BlockSpec((tm, tn), lambda j, i: (i, j)),
        out_shape=jax.ShapeDtypeStruct((tp, n_cols), F32),
        scratch_shapes=[pltpu.VMEM((tn, k) if w_is_nk else (k, tn), BF16)],
        compiler_params=_cparams(("arbitrary", "arbitrary")), name="proj",
    )(*args)


def _mm_f32_kernel(*refs, has_res, w_is_nk):
    if has_res:
        x_ref, w_ref, r_ref, o_ref = refs
    else:
        x_ref, w_ref, o_ref = refs
    contract = (((1,), (1 if w_is_nk else 0,)), ((), ()))
    acc = lax.dot_general(x_ref[...], w_ref[...], contract, preferred_element_type=F32,
                          precision=lax.Precision.HIGHEST)
    if has_res:
        acc = r_ref[...] + acc
    o_ref[...] = acc


def matmul_f32(x, w, layer, n_cols, tn, res=None, w_is_nk=False):
    m, k = x.shape
    if w_is_nk:
        w_spec = pl.BlockSpec((None, tn, k), lambda j: (layer, j, 0))
    else:
        w_spec = pl.BlockSpec((None, k, tn), lambda j: (layer, 0, j))
    in_specs = [pl.BlockSpec((m, k), lambda j: (0, 0)), w_spec]
    args = [x, w]
    if res is not None:
        in_specs.append(pl.BlockSpec((m, tn), lambda j: (0, j)))
        args.append(res)
    return pl.pallas_call(
        functools.partial(_mm_f32_kernel, has_res=res is not None, w_is_nk=w_is_nk),
        grid=(n_cols // tn,), in_specs=in_specs, out_specs=pl.BlockSpec((m, tn), lambda j: (0, j)),
        out_shape=jax.ShapeDtypeStruct((m, n_cols), F32),
        compiler_params=_cparams(("parallel",)), name="proj_f32",
    )(*args)


def _silu(x):
    return x * (0.5 * jnp.tanh(0.5 * x) + 0.5)


def _split2(v):
    hi = v.astype(BF16)
    lo = (v - hi.astype(F32)).astype(BF16)
    return hi, lo


def _ssd_kernel(z_ref, x_ref, b_ref, c_ref, dt_ref, conv0_ref, ssm0_ref,
                cw_ref, cb_ref, dtb_ref, alog_ref, dsk_ref, gn_ref, e_ref,
                y_ref, convo_ref, ssmo_ref,
                xpad, xs_s, xlo_s, xhi_s, xw_s, bm_s, bmt_s, cm_s, y_s, ea_s, state_s,
                *, n_valid, has_init):
    L = SSD_CHUNK
    mx = xlo_s.dtype
    prec = lax.Precision.HIGHEST if mx == F32 else None
    d_inner = xs_s.shape[1]
    gn = bm_s.shape[1]
    n_heads = d_inner // SSM_HEAD_DIM
    pair = 2 * SSM_HEAD_DIM
    gcols = d_inner // N_SSM_GROUPS
    c = pl.program_id(1)
    pad0 = V7X_SUBLANES
    keep = CONV_W - 1

    @pl.when(c == 0)
    def _():
        if has_init:
            xpad[pad0 - keep:pad0, :] = conv0_ref[...]
            for g in range(N_SSM_GROUPS):
                state_s[:, g * gcols:(g + 1) * gcols] = ssm0_ref[g * gcols:(g + 1) * gcols, :].T
        else:
            xpad[0:pad0, :] = jnp.zeros((pad0, xpad.shape[1]), F32)
            state_s[...] = jnp.zeros(state_s.shape, F32)

    xpad[pad0:pad0 + L, 0:d_inner] = x_ref[...]
    xpad[pad0:pad0 + L, d_inner:d_inner + gn] = b_ref[...]
    xpad[pad0:pad0 + L, d_inner + gn:d_inner + 2 * gn] = c_ref[...]
    lane = lax.broadcasted_iota(jnp.int32, (L, 512), 1)
    even_head = (lane // SSM_HEAD_DIM) % 2 == 0
    strip = 512
    for s in range(xpad.shape[1] // strip):
        cs = slice(s * strip, (s + 1) * strip)
        acc = jnp.broadcast_to(cb_ref[:, cs], (L, strip))
        for k in range(CONV_W):
            acc = acc + cw_ref[k:k + 1, cs] * xpad[pad0 - keep + k:pad0 - keep + k + L, cs]
        v = _silu(acc)
        if s * strip < d_inner:
            xs_s[:, cs] = v
            vb = v.astype(mx)
            xlo_s[:, cs] = jnp.where(even_head, vb, jnp.zeros_like(vb))
            xhi_s[:, cs] = jnp.where(even_head, jnp.zeros_like(vb), vb)
        elif s * strip < d_inner + gn:
            o = s * strip - d_inner
            bm_s[:, o:o + strip] = v.astype(mx)
            for q in range(strip // D_STATE):
                gi = o // D_STATE + q
                bmt_s[:, gi * L:(gi + 1) * L] = v[:, q * D_STATE:(q + 1) * D_STATE].T.astype(mx)
        else:
            o = s * strip - d_inner - gn
            cm_s[:, o:o + strip] = v.astype(mx)

    tail = xpad[pad0 + n_valid - keep:pad0 + n_valid, :]
    xpad[pad0 - keep:pad0, :] = tail

    @pl.when(c == pl.num_programs(1) - 1)
    def _():
        convo_ref[...] = tail

    row = lax.broadcasted_iota(jnp.int32, (L, V7X_LANES), 0)
    dtv = dt_ref[...] + dtb_ref[...]
    dt = jnp.maximum(dtv, 0.0) + jnp.log(1.0 + jnp.exp(-jnp.abs(dtv)))
    dt = jnp.where(row < n_valid, dt, 0.0)
    a = -jnp.exp(alog_ref[...])
    adt = dt * a
    ii = lax.broadcasted_iota(jnp.int32, (L, L), 0)
    jj = lax.broadcasted_iota(jnp.int32, (L, L), 1)
    causal = ii >= jj
    tri = jnp.where(causal, 1.0, 0.0).astype(F32)
    a_cs = jnp.dot(tri, adt, preferred_element_type=F32, precision=lax.Precision.HIGHEST)
    a_cs_t = a_cs.T
    dt_t = dt.T
    ea = jnp.exp(a_cs)
    dte = dt * jnp.exp(a_cs[L - 1:L, :] - a_cs)
    both = jnp.concatenate([ea, dte], axis=0)
    hi, lo = _split2(both)
    for s in range(d_inner // 512):
        cs = slice(s * 512, (s + 1) * 512)
        if mx == F32:
            ex = jnp.dot(both, e_ref[:, cs].astype(F32), preferred_element_type=F32, precision=prec)
        else:
            ex = (jnp.dot(hi, e_ref[:, cs], preferred_element_type=F32)
                  + jnp.dot(lo, e_ref[:, cs], preferred_element_type=F32))
        ea_s[:, cs] = ex[0:L]
        xw_s[:, cs] = (xs_s[:, cs] * ex[L:2 * L]).astype(mx)

    for g in range(N_SSM_GROUPS):
        gs = slice(g * gcols, (g + 1) * gcols)
        ns = slice(g * D_STATE, (g + 1) * D_STATE)
        cg = cm_s[:, ns]
        cb = lax.dot_general(cg, bm_s[:, ns], (((1,), (1,)), ((), ())), preferred_element_type=F32,
                             precision=prec)
        st = state_s[:, gs]
        y_off = jnp.dot(cg, st.astype(mx), preferred_element_type=F32, precision=prec)
        for q in range(gcols // pair):
            col = slice(g * gcols + q * pair, g * gcols + (q + 1) * pair)
            acc = None
            for hh, xsrc in ((0, xlo_s), (1, xhi_s)):
                h = (g * gcols + q * pair) // SSM_HEAD_DIM + hh
                seg = a_cs[:, h:h + 1] - a_cs_t[h:h + 1, :]
                dec = jnp.exp(jnp.where(causal, seg, NEG_INF))
                m = (cb * dec * dt_t[h:h + 1, :]).astype(mx)
                part = jnp.dot(m, xsrc[:, col], preferred_element_type=F32, precision=prec)
                acc = part if acc is None else acc + part
            lc = slice(q * pair, (q + 1) * pair)
            y_s[:, col] = acc + y_off[:, lc] * ea_s[:, col] + dsk_ref[:, col] * xs_s[:, col]
        new = st * ea_s[L - 1:L, gs] + jnp.dot(bmt_s[:, g * L:(g + 1) * L], xw_s[:, gs],
                                               preferred_element_type=F32, precision=prec)
        state_s[:, gs] = new

    for g in range(N_SSM_GROUPS):
        gs = slice(g * gcols, (g + 1) * gcols)
        v = y_s[:, gs] * _silu(z_ref[:, gs])
        v = v * lax.rsqrt(jnp.mean(v * v, axis=-1, keepdims=True) + RMS_EPS)
        y_ref[:, gs] = (v * gn_ref[:, gs]).astype(y_ref.dtype)

    @pl.when(c == pl.num_programs(1) - 1)
    def _():
        for g in range(N_SSM_GROUPS):
            ssmo_ref[g * gcols:(g + 1) * gcols, :] = state_s[:, g * gcols:(g + 1) * gcols].T


def ssd_mixer(zxbc, dt_raw, row0, n_seq, n_chunks, n_valid, conv0, ssm0, prm, layer, d_inner, gn,
              precise=False):
    L = SSD_CHUNK
    assert row0 % L == 0
    rb = row0 // L
    mx = F32 if precise else BF16
    conv_dim = d_inner + 2 * gn
    has_init = conv0 is not None
    if not has_init:
        conv0 = jnp.zeros((n_seq, CONV_W - 1, conv_dim), F32)
        ssm0 = jnp.zeros((n_seq, 8, D_STATE), F32)
    n_heads = d_inner // SSM_HEAD_DIM
    rows = n_seq * n_chunks * L

    def rmap(b, c):
        return rb + b * n_chunks + c

    zb = d_inner // gn
    in_specs = [
        pl.BlockSpec((L, d_inner), lambda b, c: (rmap(b, c), 0)),
        pl.BlockSpec((L, d_inner), lambda b, c: (rmap(b, c), 1)),
        pl.BlockSpec((L, gn), lambda b, c: (rmap(b, c), 2 * zb)),
        pl.BlockSpec((L, gn), lambda b, c: (rmap(b, c), 2 * zb + 1)),
        pl.BlockSpec((L, V7X_LANES), lambda b, c: (rmap(b, c), 0)),
        pl.BlockSpec((None, CONV_W - 1, conv_dim), lambda b, c: (b, 0, 0)),
        pl.BlockSpec((None,) + ssm0.shape[1:], lambda b, c: (b, 0, 0)),
        pl.BlockSpec((None, CONV_W, conv_dim), lambda b, c: (layer, 0, 0)),
        pl.BlockSpec((None, 1, conv_dim), lambda b, c: (layer, 0, 0)),
        pl.BlockSpec((None, 1, V7X_LANES), lambda b, c: (layer, 0, 0)),
        pl.BlockSpec((None, 1, V7X_LANES), lambda b, c: (layer, 0, 0)),
        pl.BlockSpec((None, 1, d_inner), lambda b, c: (layer, 0, 0)),
        pl.BlockSpec((None, 1, d_inner), lambda b, c: (layer, 0, 0)),
        pl.BlockSpec((V7X_LANES, d_inner), lambda b, c: (0, 0)),
    ]
    out_shape = [
        jax.ShapeDtypeStruct((rows, d_inner), mx),
        jax.ShapeDtypeStruct((n_seq, CONV_W - 1, conv_dim), F32),
        jax.ShapeDtypeStruct((n_seq, n_heads * SSM_HEAD_DIM, D_STATE), F32),
    ]
    out_specs = [
        pl.BlockSpec((L, d_inner), lambda b, c: (b * n_chunks + c, 0)),
        pl.BlockSpec((None, CONV_W - 1, conv_dim), lambda b, c: (b, 0, 0)),
        pl.BlockSpec((None, n_heads * SSM_HEAD_DIM, D_STATE), lambda b, c: (b, 0, 0)),
    ]
    scratch = [
        pltpu.VMEM((L + V7X_SUBLANES, conv_dim), F32),
        pltpu.VMEM((L, d_inner), F32),
        pltpu.VMEM((L, d_inner), mx),
        pltpu.VMEM((L, d_inner), mx),
        pltpu.VMEM((L, d_inner), mx),
        pltpu.VMEM((L, gn), mx),
        pltpu.VMEM((D_STATE, N_SSM_GROUPS * L), mx),
        pltpu.VMEM((L, gn), mx),
        pltpu.VMEM((L, d_inner), F32),
        pltpu.VMEM((L, d_inner), F32),
        pltpu.VMEM((D_STATE, d_inner), F32),
    ]
    return pl.pallas_call(
        functools.partial(_ssd_kernel, n_valid=n_valid, has_init=has_init),
        grid=(n_seq, n_chunks), in_specs=in_specs, out_specs=out_specs, out_shape=out_shape,
        scratch_shapes=scratch,
        compiler_params=_cparams(("arbitrary", "arbitrary")), name="ssd_mixer",
    )(zxbc, zxbc, zxbc, zxbc, dt_raw, conv0, ssm0,
      prm["conv_w"], prm["conv_b"], prm["dt_bias"], prm["a_log"], prm["d_skip"], prm["gate_norm"],
      prm["expand"])


def _attn_prompt_kernel(q0_ref, q1_ref, q2_ref, k_ref, v_ref, o_ref, *scratch, seq):
    og_s, lse_s = scratch[:len(DILATIONS)], scratch[len(DILATIONS):]
    QB = ATTN_Q_BLOCK
    W = N_DIL_KEYS - 1
    scale = ATT_HEAD_DIM ** -0.5
    dist_c = (lax.broadcasted_iota(jnp.int32, (QB, QB), 0) - lax.broadcasted_iota(jnp.int32, (QB, QB), 1))
    cur_ok = jnp.logical_and(dist_c >= 0, dist_c <= W)
    kq = lax.broadcasted_iota(jnp.int32, (QB, W), 1) - lax.broadcasted_iota(jnp.int32, (QB, W), 0)

    def scores(q, k_rows, ok):
        s = lax.dot_general(q, k_rows.astype(BF16), (((1,), (1,)), ((), ())),
                            preferred_element_type=F32) * scale
        return jnp.where(ok, s, NEG_INF)

    for g, (q_ref, d) in enumerate(zip((q0_ref, q1_ref, q2_ref), DILATIONS)):
        nb_r = seq // d // QB

        def body(idx, carry, g=g, q_ref=q_ref, d=d, nb_r=nb_r):
            r = idx // nb_r
            mb = idx % nb_r
            start = r + d * QB * mb
            rows = pl.ds(start, QB, stride=d) if d > 1 else pl.ds(start, QB)
            q = q_ref[rows, :].astype(BF16)
            s_c = scores(q, k_ref[rows, :], cur_ok)
            m = jnp.max(s_c, axis=-1, keepdims=True)
            if nb_r > 1:
                pstart = jnp.maximum(start - d * W, r)
                prev_min = jnp.where(mb > 0, 0, W)
                prows = pl.ds(pstart, W, stride=d) if d > 1 else pl.ds(pstart, W)
                s_p = scores(q, k_ref[prows, :], kq >= prev_min)
                m = jnp.maximum(m, jnp.max(s_p, axis=-1, keepdims=True))
            p_c = jnp.exp(s_c - m)
            den = jnp.sum(p_c, axis=-1, keepdims=True)
            pv = jnp.dot(p_c.astype(BF16), v_ref[rows, :].astype(BF16), preferred_element_type=F32)
            if nb_r > 1:
                p_p = jnp.exp(s_p - m)
                den = den + jnp.sum(p_p, axis=-1, keepdims=True)
                pv = pv + jnp.dot(p_p.astype(BF16), v_ref[prows, :].astype(BF16), preferred_element_type=F32)
            og_s[g][rows, :] = pv / den
            lse_s[g][rows, :] = jnp.broadcast_to(m + jnp.log(den), (QB, ATT_HEAD_DIM))
            return carry

        n_blk = seq // QB
        lax.fori_loop(0, n_blk, body, 0, unroll=min(ATTN_UNROLL, n_blk))

    QB = 128

    def merge(i, carry):
        rows = pl.ds(pl.multiple_of(i * QB, QB), QB)
        lse = [lse_s[g][rows, :] for g in range(len(DILATIONS))]
        top = functools.reduce(jnp.maximum, lse)
        w = [jnp.exp(x - top) for x in lse]
        num = functools.reduce(lambda a, b: a + b, [w[g] * og_s[g][rows, :] for g in range(len(DILATIONS))])
        o_ref[rows, :] = num / functools.reduce(lambda a, b: a + b, w)
        return carry

    lax.fori_loop(0, seq // QB, merge, 0, unroll=ATTN_UNROLL)


def attn_prompt(q, kv, n_seq, seq):
    hd = ATT_HEAD_DIM
    nh = N_KV_HEADS

    def qspec(g):
        return pl.BlockSpec((seq, hd), lambda b, h: (b, g * nh + h))

    return pl.pallas_call(
        functools.partial(_attn_prompt_kernel, seq=seq),
        grid=(n_seq, nh),
        in_specs=[qspec(0), qspec(1), qspec(2),
                  pl.BlockSpec((seq, hd), lambda b, h: (b, h)),
                  pl.BlockSpec((seq, hd), lambda b, h: (b, nh + h))],
        out_specs=pl.BlockSpec((seq, hd), lambda b, h: (b, h)),
        out_shape=jax.ShapeDtypeStruct((n_seq * seq, nh * hd), F32),
        scratch_shapes=[pltpu.VMEM((seq, hd), F32)] * (2 * len(DILATIONS)),
        compiler_params=_cparams(("parallel", "parallel")), name="attn_prompt",
    )(q, q, q, kv, kv)


def _attn_sample_kernel(q_ref, kn_ref, vn_ref, kc_ref, vc_ref, o_ref, *, n_new, past):
    QR = q_ref.shape[0]
    scale = ATT_HEAD_DIM ** -0.5
    q = q_ref[...].astype(BF16)

    def dist_ok(dist, dil):
        return jnp.logical_and(jnp.logical_and(dist >= 0, dist <= dil * (N_DIL_KEYS - 1)),
                               jnp.bitwise_and(dist, dil - 1) == 0)

    def mask(n_keys, key0):
        rr = lax.broadcasted_iota(jnp.int32, (QR, n_keys), 0)
        kpos = lax.broadcasted_iota(jnp.int32, (QR, n_keys), 1) + key0
        g = rr // n_new
        qpos = past + rr % n_new
        dil = jnp.where(g == 0, DILATIONS[0], jnp.where(g == 1, DILATIONS[1], DILATIONS[2]))
        ok = dist_ok(qpos - kpos, dil)
        return jnp.logical_and(ok, g < len(DILATIONS))

    ok_c = mask(past, 0)
    ok_n = jnp.logical_and(mask(kn_ref.shape[0], past),
                           lax.broadcasted_iota(jnp.int32, (QR, kn_ref.shape[0]), 1) < n_new)
    head_rows = pl.ds(pl.program_id(1), past, stride=N_KV_HEADS)
    s_c = lax.dot_general(q, kc_ref[head_rows, :].astype(BF16), (((1,), (1,)), ((), ())),
                          preferred_element_type=F32) * scale
    s_n = lax.dot_general(q, kn_ref[...].astype(BF16), (((1,), (1,)), ((), ())),
                          preferred_element_type=F32) * scale
    s_c = jnp.where(ok_c, s_c, NEG_INF)
    s_n = jnp.where(ok_n, s_n, NEG_INF)
    m = jnp.maximum(jnp.max(s_c, axis=-1, keepdims=True), jnp.max(s_n, axis=-1, keepdims=True))
    p_c = jnp.exp(s_c - m)
    p_n = jnp.exp(s_n - m)
    l = jnp.sum(p_c, axis=-1, keepdims=True) + jnp.sum(p_n, axis=-1, keepdims=True)
    pv = (jnp.dot(p_c.astype(BF16), vc_ref[head_rows, :].astype(BF16), preferred_element_type=F32)
          + jnp.dot(p_n.astype(BF16), vn_ref[...].astype(BF16), preferred_element_type=F32))
    ms = [m[g * n_new:(g + 1) * n_new] for g in range(len(DILATIONS))]
    m_all = functools.reduce(jnp.maximum, ms)
    num = jnp.zeros((n_new, ATT_HEAD_DIM), F32)
    den = jnp.zeros((n_new, 1), F32)
    for g in range(len(DILATIONS)):
        w = jnp.exp(ms[g] - m_all)
        num = num + w * pv[g * n_new:(g + 1) * n_new]
        den = den + w * l[g * n_new:(g + 1) * n_new]
    o_ref[...] = num / den


def attn_sample(q_s, k_new, v_new, cache_k, cache_v, n_new):
    nb, nh, qr, hd = q_s.shape
    past = cache_k.shape[1] // nh
    slab = pl.BlockSpec((None, past * nh, hd), lambda b, h: (b, 0, 0))
    return pl.pallas_call(
        functools.partial(_attn_sample_kernel, n_new=n_new, past=past),
        grid=(nb, nh),
        in_specs=[pl.BlockSpec((None, None, qr, hd), lambda b, h: (b, h, 0, 0)),
                  pl.BlockSpec((None, None, k_new.shape[2], hd), lambda b, h: (b, h, 0, 0)),
                  pl.BlockSpec((None, None, k_new.shape[2], hd), lambda b, h: (b, h, 0, 0)),
                  slab, slab],
        out_specs=pl.BlockSpec((None, None, n_new, hd), lambda b, h: (b, h, 0, 0)),
        out_shape=jax.ShapeDtypeStruct((nb, nh, n_new, hd), F32),
        compiler_params=_cparams(("arbitrary", "arbitrary")), name="attn_sample",
    )(q_s, k_new, v_new, cache_k, cache_v)


def _router_kernel(h_ref, g_ref, w_ref, b_ref, xn_ref, idx_ref, wt_ref):
    h = h_ref[...]
    xn = h * lax.rsqrt(jnp.mean(h * h, axis=-1, keepdims=True) + RMS_EPS) * g_ref[...]
    xn_ref[...] = xn.astype(xn_ref.dtype)
    x_hi, x_lo = _split2(xn)
    w = w_ref[...]
    w_hi, w_lo = _split2(w)
    logits = (jnp.dot(x_hi, w_hi, preferred_element_type=F32) + jnp.dot(x_lo, w_hi, preferred_element_type=F32)
              + jnp.dot(x_hi, w_lo, preferred_element_type=F32)) + b_ref[...]
    tm = logits.shape[0]
    lane = lax.broadcasted_iota(jnp.int32, logits.shape, 1)
    big = jnp.int32(V7X_LANES)
    is_g = lane < N_EXPERT_GROUPS
    gl = jnp.where(is_g, logits, NEG_INF)
    gmax = jnp.max(gl, axis=-1, keepdims=True)
    g_idx = jnp.min(jnp.where(jnp.logical_and(is_g, gl == gmax), lane, big), axis=-1, keepdims=True)
    g_w = 1.0 / jnp.sum(jnp.where(is_g, jnp.exp(gl - gmax), 0.0), axis=-1, keepdims=True)
    lo = N_EXPERT_GROUPS + g_idx * EXPERTS_PER_GROUP
    in_grp = jnp.logical_and(lane >= lo, lane < lo + EXPERTS_PER_GROUP)
    el = jnp.where(in_grp, logits, NEG_INF)
    v1 = jnp.max(el, axis=-1, keepdims=True)
    i1 = jnp.min(jnp.where(jnp.logical_and(in_grp, el == v1), lane, big), axis=-1, keepdims=True)
    el2 = jnp.where(lane == i1, NEG_INF, el)
    v2 = jnp.max(el2, axis=-1, keepdims=True)
    i2 = jnp.min(jnp.where(jnp.logical_and(in_grp, jnp.logical_and(el2 == v2, lane != i1)), lane, big),
                 axis=-1, keepdims=True)
    e2 = jnp.exp(v2 - v1)
    w1 = g_w / (1.0 + e2)
    w2 = g_w * e2 / (1.0 + e2)
    col = lax.broadcasted_iota(jnp.int32, (tm, 2), 1)
    idx_ref[...] = jnp.where(col == 0, i1, i2) - N_EXPERT_GROUPS
    wt_ref[...] = jnp.where(col == 0, w1, w2)


def router(h, gain, w_cat, b_cat):
    tp, d = h.shape
    tm = TOKEN_TILE
    return pl.pallas_call(
        _router_kernel, grid=(tp // tm,),
        in_specs=[pl.BlockSpec((tm, d), lambda i: (i, 0)),
                  pl.BlockSpec((1, d), lambda i: (0, 0)),
                  pl.BlockSpec((d, V7X_LANES), lambda i: (0, 0)),
                  pl.BlockSpec((1, V7X_LANES), lambda i: (0, 0))],
        out_specs=[pl.BlockSpec((tm, d), lambda i: (i, 0)),
                   pl.BlockSpec((tm, 2), lambda i: (i, 0)),
                   pl.BlockSpec((tm, 2), lambda i: (i, 0))],
        out_shape=[jax.ShapeDtypeStruct((tp, d), BF16),
                   jax.ShapeDtypeStruct((tp, 2), jnp.int32),
                   jax.ShapeDtypeStruct((tp, 2), F32)],
        compiler_params=_cparams(("parallel",)), name="router",
    )(h, gain, w_cat, b_cat)


RUN_ALIGN = 16
LOCAL_ROWS = 2 * TOKEN_TILE + N_EXPERTS * RUN_ALIGN
GRANULES = LOCAL_ROWS // RUN_ALIGN


def _for_pieces(n, chunk, fn):
    n_big = n // chunk
    lax.fori_loop(0, n_big, lambda j, c: (fn(j * chunk, chunk), c)[1], 0)
    done = n_big * chunk
    size = chunk // 2
    while size >= RUN_ALIGN:
        hit = (n & size) != 0
        pl.when(hit)(functools.partial(fn, done, size))
        done = done + jnp.where(hit, size, 0)
        size //= 2


def _for_granules(tot_ref, gran_ref, blk, fn):
    def one(j, carry):
        fn(pl.multiple_of(j * RUN_ALIGN, RUN_ALIGN), pl.multiple_of(gran_ref[blk * GRANULES + j], RUN_ALIGN))
        return carry
    lax.fori_loop(0, tot_ref[blk] // RUN_ALIGN, one, 0)


def _wait_rows(tot_ref, blk, wait_rows):
    _for_pieces(tot_ref[blk], LOCAL_ROWS // 2, lambda off, size: wait_rows(size))


def _dispatch_kernel(tot_ref, gran_ref, fill0_ref, filln_ref,
                     x_ref, src_ref, xs_hbm, loc, zeros, sem, zsem):
    b = pl.program_id(0)
    nb = pl.num_programs(0)
    slot = b % 2

    def copy(s, l, g, size):
        return pltpu.make_async_copy(loc.at[s, pl.ds(l, size)], xs_hbm.at[pl.ds(g, size)], sem.at[s])

    def wait_block(blk, s):
        _wait_rows(tot_ref, blk, lambda size: copy(s, 0, 0, size).wait())

    @pl.when(b >= 2)
    def _():
        wait_block(b - 2, slot)

    tok = lax.broadcasted_iota(jnp.int32, (LOCAL_ROWS, TOKEN_TILE), 1)
    pick = jnp.where(jnp.right_shift(src_ref[...], 1) == tok, 1.0, 0.0).astype(BF16)
    loc[slot] = jnp.dot(pick, x_ref[...], preferred_element_type=F32).astype(BF16)
    _for_granules(tot_ref, gran_ref, b, lambda l, g: copy(slot, l, g, RUN_ALIGN).start())

    @pl.when(b == nb - 1)
    def _():
        @pl.when(b >= 1)
        def _():
            wait_block(b - 1, 1 - slot)
        wait_block(b, slot)
        zeros[...] = jnp.zeros(zeros.shape, zeros.dtype)

        def fill(op):
            def per_expert(e, carry):
                f0 = fill0_ref[e]
                _for_pieces(filln_ref[e], MOE_ROW_TILE // 2, lambda off, size: op(pltpu.make_async_copy(
                    zeros.at[pl.ds(0, size)], xs_hbm.at[pl.ds(pl.multiple_of(f0 + off, RUN_ALIGN), size)], zsem)))
                return carry
            lax.fori_loop(0, fill0_ref.shape[0], per_expert, 0)

        fill(lambda c: c.start())
        fill(lambda c: c.wait())


def moe_dispatch(xn, plan):
    tp, d = xn.shape
    nb = tp // TOKEN_TILE
    grid_spec = pltpu.PrefetchScalarGridSpec(
        num_scalar_prefetch=4, grid=(nb,),
        in_specs=[pl.BlockSpec((TOKEN_TILE, d), lambda b, *_: (b, 0)),
                  pl.BlockSpec((None, LOCAL_ROWS, 1), lambda b, *_: (b, 0, 0))],
        out_specs=pl.BlockSpec(memory_space=pl.ANY),
        scratch_shapes=[pltpu.VMEM((2, LOCAL_ROWS, d), BF16), pltpu.VMEM((MOE_ROW_TILE // 2, d), BF16),
                        pltpu.SemaphoreType.DMA((2,)), pltpu.SemaphoreType.DMA],
    )
    return pl.pallas_call(
        _dispatch_kernel, grid_spec=grid_spec,
        out_shape=jax.ShapeDtypeStruct((plan["n_rows"], d), BF16),
        compiler_params=_cparams(("arbitrary",)), name="moe_dispatch",
    )(plan["tot"], plan["gran"], plan["fill0"], plan["filln"], xn, plan["src_col"])


def _expert_kernel(te_ref, nu_ref, first_ref, nxt_ref, par_ref,
                   xs_ref, wg_hbm, wu_hbm, wd_hbm, ys_ref,
                   wgf, wuf, wdf, wgb, wub, wdb, wsem, *, layer):
    i = pl.program_id(0)
    used = i < nu_ref[0]

    def weight_copies(e, s):
        return (pltpu.make_async_copy(wg_hbm.at[layer, e], wgf.at[s], wsem.at[s]),
                pltpu.make_async_copy(wu_hbm.at[layer, e], wuf.at[s], wsem.at[s]),
                pltpu.make_async_copy(wd_hbm.at[layer, e], wdf.at[s], wsem.at[s]))

    @pl.when(used)
    def _():
        @pl.when(first_ref[i] == 1)
        def _():
            s = par_ref[i]

            @pl.when(i == 0)
            def _():
                for c in weight_copies(te_ref[0], s):
                    c.start()

            for c in weight_copies(te_ref[i], s):
                c.wait()

            @pl.when(nxt_ref[i] >= 0)
            def _():
                for c in weight_copies(nxt_ref[i], 1 - s):
                    c.start()

            wgb[...] = wgf[s].astype(BF16)
            wub[...] = wuf[s].astype(BF16)
            wdb[...] = wdf[s].astype(BF16)

        x = xs_ref[...]
        hg = jnp.dot(x, wgb[...], preferred_element_type=F32)
        hu = jnp.dot(x, wub[...], preferred_element_type=F32)
        act = (_silu(hg) * hu).astype(BF16)
        ys_ref[...] = jnp.dot(act, wdb[...], preferred_element_type=F32).astype(ys_ref.dtype)

    @pl.when(jnp.logical_not(used))
    def _():
        ys_ref[...] = jnp.zeros(ys_ref.shape, ys_ref.dtype)


def expert_ffn(xs, plan, w_gate, w_up, w_down, layer):
    n_rows, d = xs.shape
    tm = MOE_ROW_TILE
    de = w_gate.shape[-1]

    grid_spec = pltpu.PrefetchScalarGridSpec(
        num_scalar_prefetch=5, grid=(n_rows // tm,),
        in_specs=[pl.BlockSpec((tm, d), lambda i, te, nu, *_: (jnp.minimum(i, nu[0] - 1), 0)),
                  pl.BlockSpec(memory_space=pl.ANY), pl.BlockSpec(memory_space=pl.ANY),
                  pl.BlockSpec(memory_space=pl.ANY)],
        out_specs=pl.BlockSpec((tm, d), lambda i, *_: (i, 0)),
        scratch_shapes=[pltpu.VMEM((2, d, de), F32), pltpu.VMEM((2, d, de), F32), pltpu.VMEM((2, de, d), F32),
                        pltpu.VMEM((d, de), BF16), pltpu.VMEM((d, de), BF16), pltpu.VMEM((de, d), BF16),
                        pltpu.SemaphoreType.DMA((2,))],
    )
    return pl.pallas_call(
        functools.partial(_expert_kernel, layer=layer), grid_spec=grid_spec,
        out_shape=jax.ShapeDtypeStruct((n_rows, d), BF16),
        compiler_params=_cparams(("arbitrary",)), name="expert_ffn",
    )(plan["tile_expert"], plan["n_used"], plan["tile_first"], plan["tile_next"], plan["tile_slot"],
      xs, w_gate, w_up, w_down)


def _combine_kernel(tot_ref, gran_ref,
                    ys_hbm, src_ref, wt_ref, h_ref, g_ref, *rest, n_out, emit_sum, head_blocks):
    outs, (buf, sem) = rest[:-2], rest[-2:]
    b = pl.program_id(0)
    nb = pl.num_programs(0)
    slot = b % 2

    def copy(s, l, g, size):
        return pltpu.make_async_copy(ys_hbm.at[pl.ds(g, size)], buf.at[s, pl.ds(l, size)], sem.at[s])

    def fetch(blk, s):
        _for_granules(tot_ref, gran_ref, blk, lambda l, g: copy(s, l, g, RUN_ALIGN).start())

    @pl.when(b == 0)
    def _():
        buf[...] = jnp.zeros(buf.shape, buf.dtype)
        fetch(0, 0)

    @pl.when(b + 1 < nb)
    def _():
        fetch(b + 1, 1 - slot)

    _wait_rows(tot_ref, b, lambda size: copy(slot, 0, 0, size).wait())
    y = buf[slot]
    tok2 = 2 * lax.broadcasted_iota(jnp.int32, (TOKEN_TILE, LOCAL_ROWS), 0)
    src = src_ref[...]
    h = h_ref[...]
    for k in range(2):
        pick = jnp.where(src == tok2 + k, 1.0, 0.0).astype(BF16)
        h = h + wt_ref[:, k:k + 1] * jnp.dot(pick, y, preferred_element_type=F32)
    if emit_sum:
        outs[0][...] = h
        outs = outs[1:]
    xn = h * lax.rsqrt(jnp.mean(h * h, axis=-1, keepdims=True) + RMS_EPS)
    if head_blocks is not None:
        val = xn * g_ref[0:1, :]

        @pl.when(b < head_blocks)
        def _():
            outs[0][...] = val

        @pl.when(b >= head_blocks)
        def _():
            outs[1][...] = val
        return
    for k in range(n_out):
        outs[k][...] = (xn * g_ref[k:k + 1, :]).astype(outs[k].dtype)


def moe_combine_norm(ys, plan, wts, h, gains, out_dtype, emit_sum, head_blocks=None):
    tp, d = h.shape
    nb = tp // TOKEN_TILE
    n_out = gains.shape[0]
    tok_spec = pl.BlockSpec((TOKEN_TILE, d), lambda b, *_: (b, 0))
    n_res = n_out + (1 if emit_sum else 0)
    if head_blocks is not None:
        assert n_out == 1 and not emit_sum and head_blocks == nb - 1
        out_specs = [pl.BlockSpec((TOKEN_TILE, d), lambda b, *_: (jnp.minimum(b, head_blocks - 1), 0)),
                     pl.BlockSpec((TOKEN_TILE, d), lambda b, *_: (0, 0))]
        out_shape = [jax.ShapeDtypeStruct((head_blocks * TOKEN_TILE, d), out_dtype),
                     jax.ShapeDtypeStruct((TOKEN_TILE, d), out_dtype)]
    else:
        out_specs = [tok_spec] * n_res
        out_shape = ([jax.ShapeDtypeStruct((tp, d), F32)] if emit_sum else []) + [
            jax.ShapeDtypeStruct((tp, d), out_dtype)] * n_out
    grid_spec = pltpu.PrefetchScalarGridSpec(
        num_scalar_prefetch=2, grid=(nb,),
        in_specs=[pl.BlockSpec(memory_space=pl.ANY),
                  pl.BlockSpec((None, 1, LOCAL_ROWS), lambda b, *_: (b, 0, 0)),
                  pl.BlockSpec((TOKEN_TILE, 2), lambda b, *_: (b, 0)),
                  tok_spec,
                  pl.BlockSpec((n_out, d), lambda b, *_: (0, 0))],
        out_specs=out_specs,
        scratch_shapes=[pltpu.VMEM((2, LOCAL_ROWS, d), BF16), pltpu.SemaphoreType.DMA((2,))],
    )
    return pl.pallas_call(
        functools.partial(_combine_kernel, n_out=n_out, emit_sum=emit_sum, head_blocks=head_blocks),
        grid_spec=grid_spec,
        out_shape=out_shape, compiler_params=_cparams(("arbitrary",)), name="moe_combine_norm",
    )(plan["tot"], plan["gran"], ys, plan["src_row"], wts, h, gains)


def moe_plan(idx, tp):
    i32 = jnp.int32
    tm = MOE_ROW_TILE
    nb = tp // TOKEN_TILE
    n_asg = 2 * TOKEN_TILE
    n_tiles = -(-(2 * tp + nb * N_EXPERTS * (RUN_ALIGN - 1) + N_EXPERTS * (tm - 1)) // tm)
    e_blk = idx.reshape(nb, n_asg)
    onehot = (e_blk[:, :, None] == jnp.arange(N_EXPERTS, dtype=i32)).astype(i32)
    tril = (jnp.arange(n_asg)[:, None] >= jnp.arange(n_asg)[None, :]).astype(F32)
    csum = jnp.einsum("ij,bje->bie", tril, onehot.astype(F32)).astype(i32)
    rank = jnp.sum(onehot * csum, axis=2) - 1
    plen = (csum[:, -1, :] + RUN_ALIGN - 1) // RUN_ALIGN * RUN_ALIGN
    loff = jnp.cumsum(plen, axis=1) - plen
    lpos = jnp.sum(onehot * loff[:, None, :], axis=2) + rank
    hit = lpos[:, :, None] == jnp.arange(LOCAL_ROWS, dtype=i32)
    src = jnp.sum(jnp.where(hit, jnp.arange(1, n_asg + 1, dtype=i32)[None, :, None], 0), axis=1) - 1
    seg = jnp.sum(plen, axis=0)
    seg_pad = (seg + tm - 1) // tm * tm
    gend = jnp.cumsum(seg_pad)
    gstart = gend - seg_pad
    gdst = gstart[None, :] + jnp.cumsum(plen, axis=0) - plen
    g_row = jnp.arange(GRANULES, dtype=i32) * RUN_ALIGN
    run_of = jnp.sum(((loff + plen)[:, None, :] <= g_row[None, :, None]).astype(i32), axis=2)
    run_hot = (run_of[:, :, None] == jnp.arange(N_EXPERTS, dtype=i32)).astype(i32)
    gran = jnp.sum(run_hot * (gdst - loff)[:, None, :], axis=2) + g_row[None, :]
    tot = jnp.sum(plen, axis=1)
    n_used = gend[-1] // tm
    t_row = jnp.minimum(jnp.arange(n_tiles, dtype=i32), n_used - 1) * tm
    tile_expert = jnp.minimum(jnp.sum(gend[None, :] <= t_row[:, None], axis=1), N_EXPERTS - 1)
    ids = jnp.arange(N_EXPERTS, dtype=i32)
    has = seg > 0
    later = jnp.logical_and(ids[None, :] > ids[:, None], has[None, :])
    nxt_e = jnp.min(jnp.where(later, ids[None, :], N_EXPERTS), axis=1)
    nxt_e = jnp.where(nxt_e < N_EXPERTS, nxt_e, -1)
    slot_e = (jnp.cumsum(has.astype(i32)) - 1) % 2
    te_hot = (tile_expert[:, None] == ids[None, :]).astype(i32)
    tile_first = jnp.concatenate([jnp.ones((1,), i32), (tile_expert[1:] != tile_expert[:-1]).astype(i32)])
    plan_tiles = {"tile_first": tile_first, "tile_next": jnp.sum(te_hot * nxt_e[None, :], axis=1),
                  "tile_slot": jnp.sum(te_hot * slot_e[None, :], axis=1)}
    return {
        **{k: v.astype(i32) for k, v in plan_tiles.items()},
        "tot": tot.astype(i32), "gran": gran.reshape(-1).astype(i32),
        "fill0": jnp.concatenate([gstart + seg, gend[-1:]]).astype(i32),
        "filln": jnp.concatenate([seg_pad - seg, n_tiles * tm - gend[-1:]]).astype(i32),
        "src_col": src[:, :, None].astype(i32), "src_row": src[:, None, :].astype(i32),
        "tile_expert": tile_expert.astype(i32), "n_used": n_used.reshape(1).astype(i32),
        "n_rows": n_tiles * tm,
    }


def kernel(x_prompt, x_sample, state_conv, state_ssm, cache_k, cache_v, a_norm, a_w_in, a_conv_w, a_conv_b, a_dt_bias, a_a_log, a_d_skip, a_gate_norm, a_w_out, kv_norm, w_kv, b_norm, b_w_q, b_w_o, moe_norm, router_group_w, router_group_b, router_expert_w, router_expert_b, expert_w_gate, expert_w_up, expert_w_down, final_norm):
    n_pb, seq, d_model = x_prompt.shape
    n_sb, n_new, _ = x_sample.shape
    n_a = a_w_in.shape[0]
    depth = moe_norm.shape[0]
    d_inner = a_w_out.shape[1]
    conv_dim = a_conv_w.shape[2]
    gn = (conv_dim - d_inner) // 2
    n_heads = d_inner // SSM_HEAD_DIM
    d_att = w_kv.shape[1] // 2
    n_p, n_s = n_pb * seq, n_sb * n_new
    lcm = MM_ROW_TILE * TOKEN_TILE // math.gcd(MM_ROW_TILE, TOKEN_TILE)
    tp = -(-(n_p + n_s) // lcm) * lcm
    L = SSD_CHUNK
    assert n_heads <= V7X_LANES and n_new <= L and n_new >= CONV_W - 1 and seq % (max(DILATIONS) * ATTN_Q_BLOCK) == 0

    def pad_rows(parts):
        rows = sum(p.shape[0] for p in parts)
        return jnp.concatenate(parts + [jnp.zeros((tp - rows,) + parts[0].shape[1:], parts[0].dtype)], axis=0)

    def lanes(v):
        return jnp.pad(v.astype(F32), ((0, 0), (0, V7X_LANES - n_heads)))[:, None, :]

    head_of_col = jnp.arange(d_inner, dtype=jnp.int32) // SSM_HEAD_DIM
    ssd_prm = {
        "conv_w": a_conv_w, "conv_b": a_conv_b[:, None, :],
        "dt_bias": lanes(a_dt_bias), "a_log": lanes(a_a_log),
        "d_skip": jnp.repeat(a_d_skip.astype(F32), SSM_HEAD_DIM, axis=1)[:, None, :],
        "gate_norm": a_gate_norm[:, None, :],
        "expand": (jnp.arange(V7X_LANES, dtype=jnp.int32)[:, None] == head_of_col[None, :]).astype(BF16),
    }
    w_dt = jnp.pad(a_w_in[:, :, 2 * d_inner + 2 * gn:], ((0, 0), (0, 0), (0, V7X_LANES - n_heads)))
    w_in_nk = jnp.swapaxes(a_w_in, 1, 2)
    r_w = jnp.concatenate([router_group_w, router_expert_w.reshape(depth, d_model, N_EXPERTS)], axis=2)
    r_w = jnp.pad(r_w, ((0, 0), (0, 0), (0, V7X_LANES - r_w.shape[2])))
    r_b = jnp.concatenate([router_group_b, router_expert_b.reshape(depth, N_EXPERTS)], axis=1)
    r_b = jnp.pad(r_b, ((0, 0), (0, V7X_LANES - r_b.shape[1])))[:, None, :]

    h = pad_rows([x_prompt.reshape(n_p, d_model), x_sample.reshape(n_s, d_model)])
    (xn,) = add_norm(h, [], a_norm[0:1], BF16, emit_sum=False)
    conv_p, ssm_p, conv_s, ssm_s = [], [], [], []
    kv = None
    for layer in range(depth):
        if layer < n_a:
            zxbc = matmul(xn, w_in_nk, layer, 2 * d_inner + 2 * gn, 1024, w_is_nk=True)
            dt_raw = matmul(xn, w_dt, layer, V7X_LANES, V7X_LANES)
            y_p, cp, sp = ssd_mixer(zxbc, dt_raw, 0, n_pb, seq // L, L, None, None, ssd_prm, layer, d_inner, gn)
            precise = layer == 0
            if precise:
                x_s = x_sample.reshape(n_s, d_model)
                (xn_s,) = add_norm(x_s, [], a_norm[0:1], F32, emit_sum=False)
                zx_s = matmul_f32(xn_s, w_in_nk, layer, 2 * d_inner + 2 * gn, 1024, w_is_nk=True)
                dt_s = matmul_f32(xn_s, w_dt, layer, V7X_LANES, V7X_LANES)
            else:
                zx_s, dt_s = zxbc[n_p:n_p + n_s], dt_raw[n_p:n_p + n_s]
            zs = jnp.pad(zx_s.reshape(n_sb, n_new, -1), ((0, 0), (0, L - n_new), (0, 0)))
            ds = jnp.pad(dt_s.reshape(n_sb, n_new, -1), ((0, 0), (0, L - n_new), (0, 0)))
            y_s, cs, ss = ssd_mixer(zs.reshape(n_sb * L, -1), ds.reshape(n_sb * L, -1), 0, n_sb, 1, n_new,
                                    state_conv[layer], state_ssm[layer].reshape(n_sb, n_heads * SSM_HEAD_DIM, D_STATE),
                                    ssd_prm, layer, d_inner, gn, precise=precise)
            conv_p.append(cp); ssm_p.append(sp); conv_s.append(cs); ssm_s.append(ss)
            y_s = y_s.reshape(n_sb, L, d_inner)[:, :n_new].reshape(n_s, d_inner)
            y = pad_rows([y_p, y_s.astype(BF16)])
            h = matmul(y, a_w_out, layer, d_model, 512, res=h)
            if precise:
                h_s = matmul_f32(y_s, a_w_out, layer, d_model, 512, res=x_s)
                h = lax.dynamic_update_slice(h, h_s, (n_p, 0))
        else:
            b = layer - n_a
            q = matmul(xn, b_w_q, b, b_w_q.shape[2], 1024)
            o_p = attn_prompt(q, kv, n_pb, seq)
            qs = q[n_p:n_p + n_s].reshape(n_sb, n_new, len(DILATIONS), N_KV_HEADS, ATT_HEAD_DIM)
            qs = qs.transpose(0, 3, 2, 1, 4).reshape(n_sb, N_KV_HEADS, len(DILATIONS) * n_new, ATT_HEAD_DIM)
            qs = jnp.pad(qs, ((0, 0), (0, 0), (0, 16 - len(DILATIONS) * n_new), (0, 0)))
            kvs = kv[n_p:n_p + n_s].reshape(n_sb, n_new, 2, N_KV_HEADS, ATT_HEAD_DIM).transpose(2, 0, 3, 1, 4)
            kvs = jnp.pad(kvs, ((0, 0), (0, 0), (0, 0), (0, V7X_LANES - n_new), (0, 0)))
            o_s = attn_sample(qs, kvs[0], kvs[1], cache_k.reshape(n_sb, -1, ATT_HEAD_DIM),
                              cache_v.reshape(n_sb, -1, ATT_HEAD_DIM), n_new)
            o = pad_rows([o_p, o_s.transpose(0, 2, 1, 3).reshape(n_s, d_att)])
            h = matmul(o, b_w_o, b, d_model, 1024, res=h)
        xm, idx, wts = router(h, moe_norm[layer:layer + 1], r_w[layer], r_b[layer])
        plan = moe_plan(idx, tp)
        ys = expert_ffn(moe_dispatch(xm, plan), plan, expert_w_gate, expert_w_up, expert_w_down, layer)
        if layer + 1 < n_a:
            h, xn = moe_combine_norm(ys, plan, wts, h, a_norm[layer + 1:layer + 2], BF16, emit_sum=True)
        elif layer + 1 == n_a:
            h, xkv, xn = moe_combine_norm(ys, plan, wts, h, jnp.stack([kv_norm, b_norm[0]]), BF16, emit_sum=True)
            kv = matmul(xkv, w_kv, None, w_kv.shape[1], 1024)
        elif layer + 1 < depth:
            h, xn = moe_combine_norm(ys, plan, wts, h, b_norm[layer + 1 - n_a:layer + 2 - n_a], BF16,
                                     emit_sum=True)
        else:
            y_head, y_tail = moe_combine_norm(ys, plan, wts, h, final_norm[None, :], F32, emit_sum=False,
                                              head_blocks=n_p // TOKEN_TILE)

    y_prompt = y_head.reshape(n_pb, seq, d_model)
    y_sample = y_tail[:n_s].reshape(n_sb, n_new, d_model)
    keep = min(max(DILATIONS) * (N_DIL_KEYS - 1), seq)
    kv_p = kv[:n_p].reshape(n_pb, seq, 2, N_KV_HEADS, ATT_HEAD_DIM)[:, seq - keep:]
    kv_s = kv[n_p:n_p + n_s].reshape(n_sb, n_new, 2, N_KV_HEADS, ATT_HEAD_DIM)
    ssm_shape = (n_heads, SSM_HEAD_DIM, D_STATE)
    return (y_prompt, y_sample,
            jnp.stack(conv_p), jnp.stack(ssm_p).reshape((n_a, n_pb) + ssm_shape),
            kv_p[:, :, 0], kv_p[:, :, 1],
            jnp.stack(conv_s), jnp.stack(ssm_s).reshape((n_a, n_sb) + ssm_shape),
            kv_s[:, :, 0], kv_s[:, :, 1])
```

```python
import functools
import math

import jax
import jax.numpy as jnp
from jax import lax
from jax.experimental import pallas as pl
from jax.experimental.pallas import tpu as pltpu

F32 = jnp.float32
BF16 = jnp.bfloat16

V7X_VMEM_BYTES = 64 * 1024 * 1024
V7X_LANES = 128
V7X_SUBLANES = 8
VMEM_LIMIT = 56 * 1024 * 1024

RMS_EPS = 1e-6
SSM_HEAD_DIM = 64
N_SSM_GROUPS = 8
D_STATE = 128
CONV_W = 4
SSD_CHUNK = 128
ATT_HEAD_DIM = 128
N_KV_HEADS = 8
DILATIONS = (1, 4, 16)
N_DIL_KEYS = 129
assert all(d & (d - 1) == 0 for d in DILATIONS)
N_EXPERT_GROUPS = 4
EXPERTS_PER_GROUP = 8
N_EXPERTS = N_EXPERT_GROUPS * EXPERTS_PER_GROUP
NEG_INF = -1e30

TOKEN_TILE = 256
MM_ROW_TILE = 768
MOE_ROW_TILE = 256
ATTN_Q_BLOCK = 256
ATTN_UNROLL = 4


def _cparams(sem):
    return pltpu.CompilerParams(dimension_semantics=sem, vmem_limit_bytes=VMEM_LIMIT)


def _norm_kernel(*refs, n_add, n_out, emit_sum):
    h = refs[0][...]
    for a in refs[1:1 + n_add]:
        h = h + a[...]
    g_ref = refs[1 + n_add]
    outs = refs[2 + n_add:]
    if emit_sum:
        outs[0][...] = h
        outs = outs[1:]
    xn = h * lax.rsqrt(jnp.mean(h * h, axis=-1, keepdims=True) + RMS_EPS)
    for k in range(n_out):
        outs[k][...] = (xn * g_ref[k:k + 1, :]).astype(outs[k].dtype)


def add_norm(h, adds, gains, out_dtype, emit_sum):
    tp, d = h.shape
    n_out = gains.shape[0]
    tm = min(TOKEN_TILE, tp)
    in_specs = [pl.BlockSpec((tm, d), lambda i: (i, 0))]
    args = [h]
    for arr, off in adds:
        in_specs.append(pl.BlockSpec((tm, d), functools.partial(lambda i, o: (i + o, 0), o=off)))
        args.append(arr)
    in_specs.append(pl.BlockSpec((n_out, d), lambda i: (0, 0)))
    args.append(gains)
    out_shape, out_specs = [], []
    if emit_sum:
        out_shape.append(jax.ShapeDtypeStruct((tp, d), F32))
        out_specs.append(pl.BlockSpec((tm, d), lambda i: (i, 0)))
    for _ in range(n_out):
        out_shape.append(jax.ShapeDtypeStruct((tp, d), out_dtype))
        out_specs.append(pl.BlockSpec((tm, d), lambda i: (i, 0)))
    return pl.pallas_call(
        functools.partial(_norm_kernel, n_add=len(adds), n_out=n_out, emit_sum=emit_sum),
        grid=(tp // tm,), in_specs=in_specs, out_specs=out_specs, out_shape=out_shape,
        compiler_params=_cparams(("parallel",)), name="add_norm",
    )(*args)


def _mm_kernel(*refs, has_res, w_is_nk):
    if has_res:
        x_ref, w_ref, r_ref, o_ref, wb_ref = refs
    else:
        x_ref, w_ref, o_ref, wb_ref = refs

    @pl.when(pl.program_id(1) == 0)
    def _():
        wb_ref[...] = w_ref[...].astype(BF16)

    contract = (((1,), (1 if w_is_nk else 0,)), ((), ()))
    acc = lax.dot_general(x_ref[...].astype(BF16), wb_ref[...], contract, preferred_element_type=F32)
    if has_res:
        acc = r_ref[...] + acc
    o_ref[...] = acc


def matmul(x, w, layer, n_cols, tn, res=None, tm=MM_ROW_TILE, w_is_nk=False):
    tp, k = x.shape
    assert tp % tm == 0 and n_cols % tn == 0
    if w_is_nk:
        w_spec = pl.BlockSpec((None, tn, k), lambda j, i: (layer, j, 0))
    elif layer is None:
        w_spec = pl.BlockSpec((k, tn), lambda j, i: (0, j))
    else:
        w_spec = pl.BlockSpec((None, k, tn), lambda j, i: (layer, 0, j))
    in_specs = [pl.BlockSpec((tm, k), lambda j, i: (i, 0)), w_spec]
    args = [x, w]
    if res is not None:
        in_specs.append(pl.BlockSpec((tm, tn), lambda j, i: (i, j)))
        args.append(res)
    return pl.pallas_call(
        functools.partial(_mm_kernel, has_res=res is not None, w_is_nk=w_is_nk),
        grid=(n_cols // tn, tp // tm), in_specs=in_specs,
        out_specs=pl.BlockSpec((tm, tn), lambda j, i: (i, j)),
        out_shape=jax.ShapeDtypeStruct((tp, n_cols), F32),
        scratch_shapes=[pltpu.VMEM((tn, k) if w_is_nk else (k, tn), BF16)],
        compiler_params=_cparams(("arbitrary", "arbitrary")), name="proj",
    )(*args)


def _mm_f32_kernel(*refs, has_res, w_is_nk):
    if has_res:
        x_ref, w_ref, r_ref, o_ref = refs
    else:
        x_ref, w_ref, o_ref = refs
    contract = (((1,), (1 if w_is_nk else 0,)), ((), ()))
    acc = lax.dot_general(x_ref[...], w_ref[...], contract, preferred_element_type=F32,
                          precision=lax.Precision.HIGHEST)
    if has_res:
        acc = r_ref[...] + acc
    o_ref[...] = acc


def matmul_f32(x, w, layer, n_cols, tn, res=None, w_is_nk=False):
    m, k = x.shape
    if w_is_nk:
        w_spec = pl.BlockSpec((None, tn, k), lambda j: (layer, j, 0))
    else:
        w_spec = pl.BlockSpec((None, k, tn), lambda j: (layer, 0, j))
    in_specs = [pl.BlockSpec((m, k), lambda j: (0, 0)), w_spec]
    args = [x, w]
    if res is not None:
        in_specs.append(pl.BlockSpec((m, tn), lambda j: (0, j)))
        args.append(res)
    return pl.pallas_call(
        functools.partial(_mm_f32_kernel, has_res=res is not None, w_is_nk=w_is_nk),
        grid=(n_cols // tn,), in_specs=in_specs, out_specs=pl.BlockSpec((m, tn), lambda j: (0, j)),
        out_shape=jax.ShapeDtypeStruct((m, n_cols), F32),
        compiler_params=_cparams(("parallel",)), name="proj_f32",
    )(*args)


def _silu(x):
    return x * (0.5 * jnp.tanh(0.5 * x) + 0.5)


def _split2(v):
    hi = v.astype(BF16)
    lo = (v - hi.astype(F32)).astype(BF16)
    return hi, lo


def _ssd_kernel(z_ref, x_ref, b_ref, c_ref, dt_ref, conv0_ref, ssm0_ref,
                cw_ref, cb_ref, dtb_ref, alog_ref, dsk_ref, gn_ref, e_ref,
                y_ref, convo_ref, ssmo_ref,
                xpad, xs_s, xlo_s, xhi_s, xw_s, bm_s, bmt_s, cm_s, y_s, ea_s, state_s, dec_s,
                *, n_valid, has_init):
    L = SSD_CHUNK
    mx = xlo_s.dtype
    prec = lax.Precision.HIGHEST if mx == F32 else None
    d_inner = xs_s.shape[1]
    gn = bm_s.shape[1]
    n_heads = d_inner // SSM_HEAD_DIM
    pair = 2 * SSM_HEAD_DIM
    gcols = d_inner // N_SSM_GROUPS
    c = pl.program_id(1)
    pad0 = V7X_SUBLANES
    keep = CONV_W - 1

    @pl.when(c == 0)
    def _():
        if has_init:
            xpad[pad0 - keep:pad0, :] = conv0_ref[...]
            for g in range(N_SSM_GROUPS):
                state_s[:, g * gcols:(g + 1) * gcols] = ssm0_ref[g * gcols:(g + 1) * gcols, :].T
        else:
            xpad[0:pad0, :] = jnp.zeros((pad0, xpad.shape[1]), F32)
            state_s[...] = jnp.zeros(state_s.shape, F32)

    xpad[pad0:pad0 + L, 0:d_inner] = x_ref[...]
    xpad[pad0:pad0 + L, d_inner:d_inner + gn] = b_ref[...]
    xpad[pad0:pad0 + L, d_inner + gn:d_inner + 2 * gn] = c_ref[...]
    strip = D_STATE
    lane = lax.broadcasted_iota(jnp.int32, (L, strip), 1)
    even_head = (lane // SSM_HEAD_DIM) % 2 == 0
    for s in range(xpad.shape[1] // strip):
        cs = slice(s * strip, (s + 1) * strip)
        rows = xpad[:, cs]
        acc = cb_ref[:, cs] + cw_ref[keep:keep + 1, cs] * rows[pad0:pad0 + L]
        for k in range(keep):
            acc = acc + cw_ref[k:k + 1, cs] * pltpu.roll(rows, keep - k, 0)[pad0:pad0 + L]
        v = _silu(acc)
        if s * strip < d_inner:
            xs_s[:, cs] = v
            vb = v.astype(mx)
            xlo_s[:, cs] = jnp.where(even_head, vb, jnp.zeros_like(vb))
            xhi_s[:, cs] = jnp.where(even_head, jnp.zeros_like(vb), vb)
        elif s * strip < d_inner + gn:
            o = s * strip - d_inner
            bm_s[:, o:o + strip] = v.astype(mx)
            for q in range(strip // D_STATE):
                gi = o // D_STATE + q
                bmt_s[:, gi * L:(gi + 1) * L] = v[:, q * D_STATE:(q + 1) * D_STATE].T.astype(mx)
        else:
            o = s * strip - d_inner - gn
            cm_s[:, o:o + strip] = v.astype(mx)

    tail = xpad[pad0 + n_valid - keep:pad0 + n_valid, :]
    xpad[pad0 - keep:pad0, :] = tail

    @pl.when(c == pl.num_programs(1) - 1)
    def _():
        convo_ref[...] = tail

    row = lax.broadcasted_iota(jnp.int32, (L, V7X_LANES), 0)
    dtv = dt_ref[...] + dtb_ref[...]
    dt = jnp.maximum(dtv, 0.0) + jnp.log(1.0 + jnp.exp(-jnp.abs(dtv)))
    dt = jnp.where(row < n_valid, dt, 0.0)
    a = -jnp.exp(alog_ref[...])
    adt = dt * a
    ii = lax.broadcasted_iota(jnp.int32, (L, L), 0)
    jj = lax.broadcasted_iota(jnp.int32, (L, L), 1)
    causal = ii >= jj
    tri = jnp.where(causal, 1.0, 0.0).astype(F32)
    a_cs = jnp.dot(tri, adt, preferred_element_type=F32, precision=lax.Precision.HIGHEST)
    dec_s[0] = a_cs
    dec_s[1] = a_cs.T
    dec_s[2] = dt.T
    dec_s[3] = jnp.where(causal, 0.0, NEG_INF).astype(F32)
    ea = jnp.exp(a_cs)
    dte = dt * jnp.exp(a_cs[L - 1:L, :] - a_cs)
    ea_p, dte_p = _split2(ea), _split2(dte)
    ex_strip = 256
    for s in range(d_inner // ex_strip):
        cs = slice(s * ex_strip, (s + 1) * ex_strip)

        def expand(v, pieces):
            if mx == F32:
                return jnp.dot(v, e_ref[:, cs].astype(F32), preferred_element_type=F32, precision=prec)
            return (jnp.dot(pieces[0], e_ref[:, cs], preferred_element_type=F32)
                    + jnp.dot(pieces[1], e_ref[:, cs], preferred_element_type=F32))

        ea_s[:, cs] = expand(ea, ea_p)
        xw_s[:, cs] = (xs_s[:, cs] * expand(dte, dte_p)).astype(mx)

    for g in range(N_SSM_GROUPS):
        ns = slice(g * D_STATE, (g + 1) * D_STATE)
        cg = cm_s[:, ns]
        cb = lax.dot_general(cg, bm_s[:, ns], (((1,), (1,)), ((), ())), preferred_element_type=F32,
                             precision=prec)
        for q in range(gcols // pair):
            col = slice(g * gcols + q * pair, g * gcols + (q + 1) * pair)
            acc = None
            for hh, xsrc in ((0, xlo_s), (1, xhi_s)):
                h = (g * gcols + q * pair) // SSM_HEAD_DIM + hh
                seg = dec_s[0, :, h:h + 1] - dec_s[1, h:h + 1, :]
                m = (cb * jnp.exp(seg + dec_s[3]) * dec_s[2, h:h + 1, :]).astype(mx)
                part = jnp.dot(m, xsrc[:, col], preferred_element_type=F32, precision=prec)
                acc = part if acc is None else acc + part
            st = state_s[:, col]
            y_off = jnp.dot(cg, st.astype(mx), preferred_element_type=F32, precision=prec)
            y_s[:, col] = acc + y_off * ea_s[:, col] + dsk_ref[:, col] * xs_s[:, col]
            state_s[:, col] = st * ea_s[L - 1:L, col] + jnp.dot(
                bmt_s[:, g * L:(g + 1) * L], xw_s[:, col], preferred_element_type=F32, precision=prec)

    for g in range(N_SSM_GROUPS):
        gs = slice(g * gcols, (g + 1) * gcols)
        v = y_s[:, gs] * _silu(z_ref[:, gs])
        v = v * lax.rsqrt(jnp.mean(v * v, axis=-1, keepdims=True) + RMS_EPS)
        y_ref[:, gs] = (v * gn_ref[:, gs]).astype(y_ref.dtype)

    @pl.when(c == pl.num_programs(1) - 1)
    def _():
        for g in range(N_SSM_GROUPS):
            ssmo_ref[g * gcols:(g + 1) * gcols, :] = state_s[:, g * gcols:(g + 1) * gcols].T


def ssd_mixer(zxbc, dt_raw, row0, n_seq, n_chunks, n_valid, conv0, ssm0, prm, layer, d_inner, gn,
              precise=False):
    L = SSD_CHUNK
    assert row0 % L == 0
    rb = row0 // L
    mx = F32 if precise else BF16
    conv_dim = d_inner + 2 * gn
    has_init = conv0 is not None
    if not has_init:
        conv0 = jnp.zeros((n_seq, CONV_W - 1, conv_dim), F32)
        ssm0 = jnp.zeros((n_seq, 8, D_STATE), F32)
    n_heads = d_inner // SSM_HEAD_DIM
    rows = n_seq * n_chunks * L

    def rmap(b, c):
        return rb + b * n_chunks + c

    zb = d_inner // gn
    in_specs = [
        pl.BlockSpec((L, d_inner), lambda b, c: (rmap(b, c), 0)),
        pl.BlockSpec((L, d_inner), lambda b, c: (rmap(b, c), 1)),
        pl.BlockSpec((L, gn), lambda b, c: (rmap(b, c), 2 * zb)),
        pl.BlockSpec((L, gn), lambda b, c: (rmap(b, c), 2 * zb + 1)),
        pl.BlockSpec((L, V7X_LANES), lambda b, c: (rmap(b, c), 0)),
        pl.BlockSpec((None, CONV_W - 1, conv_dim), lambda b, c: (b, 0, 0)),
        pl.BlockSpec((None,) + ssm0.shape[1:], lambda b, c: (b, 0, 0)),
        pl.BlockSpec((None, CONV_W, conv_dim), lambda b, c: (layer, 0, 0)),
        pl.BlockSpec((None, 1, conv_dim), lambda b, c: (layer, 0, 0)),
        pl.BlockSpec((None, 1, V7X_LANES), lambda b, c: (layer, 0, 0)),
        pl.BlockSpec((None, 1, V7X_LANES), lambda b, c: (layer, 0, 0)),
        pl.BlockSpec((None, 1, d_inner), lambda b, c: (layer, 0, 0)),
        pl.BlockSpec((None, 1, d_inner), lambda b, c: (layer, 0, 0)),
        pl.BlockSpec((V7X_LANES, d_inner), lambda b, c: (0, 0)),
    ]
    out_shape = [
        jax.ShapeDtypeStruct((rows, d_inner), mx),
        jax.ShapeDtypeStruct((n_seq, CONV_W - 1, conv_dim), F32),
        jax.ShapeDtypeStruct((n_seq, n_heads * SSM_HEAD_DIM, D_STATE), F32),
    ]
    out_specs = [
        pl.BlockSpec((L, d_inner), lambda b, c: (b * n_chunks + c, 0)),
        pl.BlockSpec((None, CONV_W - 1, conv_dim), lambda b, c: (b, 0, 0)),
        pl.BlockSpec((None, n_heads * SSM_HEAD_DIM, D_STATE), lambda b, c: (b, 0, 0)),
    ]
    scratch = [
        pltpu.VMEM((L + V7X_SUBLANES, conv_dim), F32),
        pltpu.VMEM((L, d_inner), F32),
        pltpu.VMEM((L, d_inner), mx),
        pltpu.VMEM((L, d_inner), mx),
        pltpu.VMEM((L, d_inner), mx),
        pltpu.VMEM((L, gn), mx),
        pltpu.VMEM((D_STATE, N_SSM_GROUPS * L), mx),
        pltpu.VMEM((L, gn), mx),
        pltpu.VMEM((L, d_inner), F32),
        pltpu.VMEM((L, d_inner), F32),
        pltpu.VMEM((D_STATE, d_inner), F32),
        pltpu.VMEM((4, L, V7X_LANES), F32),
    ]
    return pl.pallas_call(
        functools.partial(_ssd_kernel, n_valid=n_valid, has_init=has_init),
        grid=(n_seq, n_chunks), in_specs=in_specs, out_specs=out_specs, out_shape=out_shape,
        scratch_shapes=scratch,
        compiler_params=_cparams(("arbitrary", "arbitrary")), name="ssd_mixer",
    )(zxbc, zxbc, zxbc, zxbc, dt_raw, conv0, ssm0,
      prm["conv_w"], prm["conv_b"], prm["dt_bias"], prm["a_log"], prm["d_skip"], prm["gate_norm"],
      prm["expand"])


def _attn_prompt_kernel(q0_ref, q1_ref, q2_ref, k_ref, v_ref, o_ref, *scratch, seq):
    og_s, lse_s = scratch[:len(DILATIONS)], scratch[len(DILATIONS):]
    QB = ATTN_Q_BLOCK
    W = N_DIL_KEYS - 1
    scale = ATT_HEAD_DIM ** -0.5
    dist_c = (lax.broadcasted_iota(jnp.int32, (QB, QB), 0) - lax.broadcasted_iota(jnp.int32, (QB, QB), 1))
    cur_ok = jnp.logical_and(dist_c >= 0, dist_c <= W)
    kq = lax.broadcasted_iota(jnp.int32, (QB, W), 1) - lax.broadcasted_iota(jnp.int32, (QB, W), 0)

    def scores(q, k_rows, ok):
        s = lax.dot_general(q, k_rows.astype(BF16), (((1,), (1,)), ((), ())),
                            preferred_element_type=F32) * scale
        return jnp.where(ok, s, NEG_INF)

    for g, (q_ref, d) in enumerate(zip((q0_ref, q1_ref, q2_ref), DILATIONS)):
        nb_r = seq // d // QB

        def body(idx, carry, g=g, q_ref=q_ref, d=d, nb_r=nb_r):
            r = idx // nb_r
            mb = idx % nb_r
            start = r + d * QB * mb
            rows = pl.ds(start, QB, stride=d) if d > 1 else pl.ds(start, QB)
            q = q_ref[rows, :].astype(BF16)
            s_c = scores(q, k_ref[rows, :], cur_ok)
            m = jnp.max(s_c, axis=-1, keepdims=True)
            if nb_r > 1:
                pstart = jnp.maximum(start - d * W, r)
                prev_min = jnp.where(mb > 0, 0, W)
                prows = pl.ds(pstart, W, stride=d) if d > 1 else pl.ds(pstart, W)
                s_p = scores(q, k_ref[prows, :], kq >= prev_min)
                m = jnp.maximum(m, jnp.max(s_p, axis=-1, keepdims=True))
            p_c = jnp.exp(s_c - m)
            den = jnp.sum(p_c, axis=-1, keepdims=True)
            pv = jnp.dot(p_c.astype(BF16), v_ref[rows, :].astype(BF16), preferred_element_type=F32)
            if nb_r > 1:
                p_p = jnp.exp(s_p - m)
                den = den + jnp.sum(p_p, axis=-1, keepdims=True)
                pv = pv + jnp.dot(p_p.astype(BF16), v_ref[prows, :].astype(BF16), preferred_element_type=F32)
            og_s[g][rows, :] = pv / den
            lse_s[g][rows, :] = jnp.broadcast_to(m + jnp.log(den), (QB, ATT_HEAD_DIM))
            return carry

        n_blk = seq // QB
        lax.fori_loop(0, n_blk, body, 0, unroll=min(ATTN_UNROLL, n_blk))

    QB = 128

    def merge(i, carry):
        rows = pl.ds(pl.multiple_of(i * QB, QB), QB)
        lse = [lse_s[g][rows, :] for g in range(len(DILATIONS))]
        top = functools.reduce(jnp.maximum, lse)
        w = [jnp.exp(x - top) for x in lse]
        num = functools.reduce(lambda a, b: a + b, [w[g] * og_s[g][rows, :] for g in range(len(DILATIONS))])
        o_ref[rows, :] = num / functools.reduce(lambda a, b: a + b, w)
        return carry

    lax.fori_loop(0, seq // QB, merge, 0, unroll=ATTN_UNROLL)


def attn_prompt(q, kv, n_seq, seq):
    hd = ATT_HEAD_DIM
    nh = N_KV_HEADS

    def qspec(g):
        return pl.BlockSpec((seq, hd), lambda b, h: (b, g * nh + h))

    return pl.pallas_call(
        functools.partial(_attn_prompt_kernel, seq=seq),
        grid=(n_seq, nh),
        in_specs=[qspec(0), qspec(1), qspec(2),
                  pl.BlockSpec((seq, hd), lambda b, h: (b, h)),
                  pl.BlockSpec((seq, hd), lambda b, h: (b, nh + h))],
        out_specs=pl.BlockSpec((seq, hd), lambda b, h: (b, h)),
        out_shape=jax.ShapeDtypeStruct((n_seq * seq, nh * hd), F32),
        scratch_shapes=[pltpu.VMEM((seq, hd), F32)] * (2 * len(DILATIONS)),
        compiler_params=_cparams(("parallel", "parallel")), name="attn_prompt",
    )(q, q, q, kv, kv)


def _attn_sample_kernel(q_ref, kn_ref, vn_ref, kc_ref, vc_ref, o_ref, *, n_new, past):
    QR = q_ref.shape[0]
    scale = ATT_HEAD_DIM ** -0.5
    q = q_ref[...].astype(BF16)

    def dist_ok(dist, dil):
        return jnp.logical_and(jnp.logical_and(dist >= 0, dist <= dil * (N_DIL_KEYS - 1)),
                               jnp.bitwise_and(dist, dil - 1) == 0)

    def mask(n_keys, key0):
        rr = lax.broadcasted_iota(jnp.int32, (QR, n_keys), 0)
        kpos = lax.broadcasted_iota(jnp.int32, (QR, n_keys), 1) + key0
        g = rr // n_new
        qpos = past + rr % n_new
        dil = jnp.where(g == 0, DILATIONS[0], jnp.where(g == 1, DILATIONS[1], DILATIONS[2]))
        ok = dist_ok(qpos - kpos, dil)
        return jnp.logical_and(ok, g < len(DILATIONS))

    ok_c = mask(past, 0)
    ok_n = jnp.logical_and(mask(kn_ref.shape[0], past),
                           lax.broadcasted_iota(jnp.int32, (QR, kn_ref.shape[0]), 1) < n_new)
    head_rows = pl.ds(pl.program_id(1), past, stride=N_KV_HEADS)
    s_c = lax.dot_general(q, kc_ref[head_rows, :].astype(BF16), (((1,), (1,)), ((), ())),
                          preferred_element_type=F32) * scale
    s_n = lax.dot_general(q, kn_ref[...].astype(BF16), (((1,), (1,)), ((), ())),
                          preferred_element_type=F32) * scale
    s_c = jnp.where(ok_c, s_c, NEG_INF)
    s_n = jnp.where(ok_n, s_n, NEG_INF)
    m = jnp.maximum(jnp.max(s_c, axis=-1, keepdims=True), jnp.max(s_n, axis=-1, keepdims=True))
    p_c = jnp.exp(s_c - m)
    p_n = jnp.exp(s_n - m)
    l = jnp.sum(p_c, axis=-1, keepdims=True) + jnp.sum(p_n, axis=-1, keepdims=True)
    pv = (jnp.dot(p_c.astype(BF16), vc_ref[head_rows, :].astype(BF16), preferred_element_type=F32)
          + jnp.dot(p_n.astype(BF16), vn_ref[...].astype(BF16), preferred_element_type=F32))
    ms = [m[g * n_new:(g + 1) * n_new] for g in range(len(DILATIONS))]
    m_all = functools.reduce(jnp.maximum, ms)
    num = jnp.zeros((n_new, ATT_HEAD_DIM), F32)
    den = jnp.zeros((n_new, 1), F32)
    for g in range(len(DILATIONS)):
        w = jnp.exp(ms[g] - m_all)
        num = num + w * pv[g * n_new:(g + 1) * n_new]
        den = den + w * l[g * n_new:(g + 1) * n_new]
    o_ref[...] = num / den


def attn_sample(q_s, k_new, v_new, cache_k, cache_v, n_new):
    nb, nh, qr, hd = q_s.shape
    past = cache_k.shape[1] // nh
    slab = pl.BlockSpec((None, past * nh, hd), lambda b, h: (b, 0, 0))
    return pl.pallas_call(
        functools.partial(_attn_sample_kernel, n_new=n_new, past=past),
        grid=(nb, nh),
        in_specs=[pl.BlockSpec((None, None, qr, hd), lambda b, h: (b, h, 0, 0)),
                  pl.BlockSpec((None, None, k_new.shape[2], hd), lambda b, h: (b, h, 0, 0)),
                  pl.BlockSpec((None, None, k_new.shape[2], hd), lambda b, h: (b, h, 0, 0)),
                  slab, slab],
        out_specs=pl.BlockSpec((None, None, n_new, hd), lambda b, h: (b, h, 0, 0)),
        out_shape=jax.ShapeDtypeStruct((nb, nh, n_new, hd), F32),
        compiler_params=_cparams(("arbitrary", "arbitrary")), name="attn_sample",
    )(q_s, k_new, v_new, cache_k, cache_v)


def _router_kernel(h_ref, g_ref, w_ref, b_ref, xn_ref, idx_ref, wt_ref):
    h = h_ref[...]
    xn = h * lax.rsqrt(jnp.mean(h * h, axis=-1, keepdims=True) + RMS_EPS) * g_ref[...]
    xn_ref[...] = xn.astype(xn_ref.dtype)
    x_hi, x_lo = _split2(xn)
    w = w_ref[...]
    w_hi, w_lo = _split2(w)
    logits = (jnp.dot(x_hi, w_hi, preferred_element_type=F32) + jnp.dot(x_lo, w_hi, preferred_element_type=F32)
              + jnp.dot(x_hi, w_lo, preferred_element_type=F32)) + b_ref[...]
    tm = logits.shape[0]
    lane = lax.broadcasted_iota(jnp.int32, logits.shape, 1)
    big = jnp.int32(V7X_LANES)
    is_g = lane < N_EXPERT_GROUPS
    gl = jnp.where(is_g, logits, NEG_INF)
    gmax = jnp.max(gl, axis=-1, keepdims=True)
    g_idx = jnp.min(jnp.where(jnp.logical_and(is_g, gl == gmax), lane, big), axis=-1, keepdims=True)
    g_w = 1.0 / jnp.sum(jnp.where(is_g, jnp.exp(gl - gmax), 0.0), axis=-1, keepdims=True)
    lo = N_EXPERT_GROUPS + g_idx * EXPERTS_PER_GROUP
    in_grp = jnp.logical_and(lane >= lo, lane < lo + EXPERTS_PER_GROUP)
    el = jnp.where(in_grp, logits, NEG_INF)
    v1 = jnp.max(el, axis=-1, keepdims=True)
    i1 = jnp.min(jnp.where(jnp.logical_and(in_grp, el == v1), lane, big), axis=-1, keepdims=True)
    el2 = jnp.where(lane == i1, NEG_INF, el)
    v2 = jnp.max(el2, axis=-1, keepdims=True)
    i2 = jnp.min(jnp.where(jnp.logical_and(in_grp, jnp.logical_and(el2 == v2, lane != i1)), lane, big),
                 axis=-1, keepdims=True)
    e2 = jnp.exp(v2 - v1)
    w1 = g_w / (1.0 + e2)
    w2 = g_w * e2 / (1.0 + e2)
    col = lax.broadcasted_iota(jnp.int32, (tm, 2), 1)
    idx_ref[...] = jnp.where(col == 0, i1, i2) - N_EXPERT_GROUPS
    wt_ref[...] = jnp.where(col == 0, w1, w2)


def router(h, gain, w_cat, b_cat):
    tp, d = h.shape
    tm = TOKEN_TILE
    return pl.pallas_call(
        _router_kernel, grid=(tp // tm,),
        in_specs=[pl.BlockSpec((tm, d), lambda i: (i, 0)),
                  pl.BlockSpec((1, d), lambda i: (0, 0)),
                  pl.BlockSpec((d, V7X_LANES), lambda i: (0, 0)),
                  pl.BlockSpec((1, V7X_LANES), lambda i: (0, 0))],
        out_specs=[pl.BlockSpec((tm, d), lambda i: (i, 0)),
                   pl.BlockSpec((tm, 2), lambda i: (i, 0)),
                   pl.BlockSpec((tm, 2), lambda i: (i, 0))],
        out_shape=[jax.ShapeDtypeStruct((tp, d), BF16),
                   jax.ShapeDtypeStruct((tp, 2), jnp.int32),
                   jax.ShapeDtypeStruct((tp, 2), F32)],
        compiler_params=_cparams(("parallel",)), name="router",
    )(h, gain, w_cat, b_cat)


RUN_ALIGN = 16
LOCAL_ROWS = 2 * TOKEN_TILE + N_EXPERTS * RUN_ALIGN
GRANULES = LOCAL_ROWS // RUN_ALIGN


def _for_pieces(n, chunk, fn):
    n_big = n // chunk
    lax.fori_loop(0, n_big, lambda j, c: (fn(j * chunk, chunk), c)[1], 0)
    done = n_big * chunk
    size = chunk // 2
    while size >= RUN_ALIGN:
        hit = (n & size) != 0
        pl.when(hit)(functools.partial(fn, done, size))
        done = done + jnp.where(hit, size, 0)
        size //= 2


def _for_granules(tot_ref, gran_ref, blk, fn):
    def one(j, carry):
        fn(pl.multiple_of(j * RUN_ALIGN, RUN_ALIGN), pl.multiple_of(gran_ref[blk * GRANULES + j], RUN_ALIGN))
        return carry
    lax.fori_loop(0, tot_ref[blk] // RUN_ALIGN, one, 0)


def _wait_rows(tot_ref, blk, wait_rows):
    _for_pieces(tot_ref[blk], LOCAL_ROWS // 2, lambda off, size: wait_rows(size))


def _dispatch_kernel(tot_ref, gran_ref, fill0_ref, filln_ref,
                     x_ref, src_ref, xs_hbm, loc, zeros, sem, zsem):
    b = pl.program_id(0)
    nb = pl.num_programs(0)
    slot = b % 2

    def copy(s, l, g, size):
        return pltpu.make_async_copy(loc.at[s, pl.ds(l, size)], xs_hbm.at[pl.ds(g, size)], sem.at[s])

    def wait_block(blk, s):
        _wait_rows(tot_ref, blk, lambda size: copy(s, 0, 0, size).wait())

    @pl.when(b >= 2)
    def _():
        wait_block(b - 2, slot)

    tok = lax.broadcasted_iota(jnp.int32, (LOCAL_ROWS, TOKEN_TILE), 1)
    pick = jnp.where(jnp.right_shift(src_ref[...], 1) == tok, 1.0, 0.0).astype(BF16)
    loc[slot] = jnp.dot(pick, x_ref[...], preferred_element_type=F32).astype(BF16)
    _for_granules(tot_ref, gran_ref, b, lambda l, g: copy(slot, l, g, RUN_ALIGN).start())

    @pl.when(b == nb - 1)
    def _():
        @pl.when(b >= 1)
        def _():
            wait_block(b - 1, 1 - slot)
        wait_block(b, slot)
        zeros[...] = jnp.zeros(zeros.shape, zeros.dtype)

        def fill(op):
            def per_expert(e, carry):
                f0 = fill0_ref[e]
                _for_pieces(filln_ref[e], MOE_ROW_TILE // 2, lambda off, size: op(pltpu.make_async_copy(
                    zeros.at[pl.ds(0, size)], xs_hbm.at[pl.ds(pl.multiple_of(f0 + off, RUN_ALIGN), size)], zsem)))
                return carry
            lax.fori_loop(0, fill0_ref.shape[0], per_expert, 0)

        fill(lambda c: c.start())
        fill(lambda c: c.wait())


def moe_dispatch(xn, plan):
    tp, d = xn.shape
    nb = tp // TOKEN_TILE
    grid_spec = pltpu.PrefetchScalarGridSpec(
        num_scalar_prefetch=4, grid=(nb,),
        in_specs=[pl.BlockSpec((TOKEN_TILE, d), lambda b, *_: (b, 0)),
                  pl.BlockSpec((None, LOCAL_ROWS, 1), lambda b, *_: (b, 0, 0))],
        out_specs=pl.BlockSpec(memory_space=pl.ANY),
        scratch_shapes=[pltpu.VMEM((2, LOCAL_ROWS, d), BF16), pltpu.VMEM((MOE_ROW_TILE // 2, d), BF16),
                        pltpu.SemaphoreType.DMA((2,)), pltpu.SemaphoreType.DMA],
    )
    return pl.pallas_call(
        _dispatch_kernel, grid_spec=grid_spec,
        out_shape=jax.ShapeDtypeStruct((plan["n_rows"], d), BF16),
        compiler_params=_cparams(("arbitrary",)), name="moe_dispatch",
    )(plan["tot"], plan["gran"], plan["fill0"], plan["filln"], xn, plan["src_col"])


def _expert_kernel(te_ref, nu_ref, first_ref, nxt_ref, par_ref,
                   xs_ref, wg_hbm, wu_hbm, wd_hbm, ys_ref,
                   wgf, wuf, wdf, wgb, wub, wdb, wsem, *, layer):
    i = pl.program_id(0)
    used = i < nu_ref[0]

    def weight_copies(e, s):
        return (pltpu.make_async_copy(wg_hbm.at[layer, e], wgf.at[s], wsem.at[s]),
                pltpu.make_async_copy(wu_hbm.at[layer, e], wuf.at[s], wsem.at[s]),
                pltpu.make_async_copy(wd_hbm.at[layer, e], wdf.at[s], wsem.at[s]))

    @pl.when(used)
    def _():
        @pl.when(first_ref[i] == 1)
        def _():
            s = par_ref[i]

            @pl.when(i == 0)
            def _():
                for c in weight_copies(te_ref[0], s):
                    c.start()

            for c in weight_copies(te_ref[i], s):
                c.wait()

            @pl.when(nxt_ref[i] >= 0)
            def _():
                for c in weight_copies(nxt_ref[i], 1 - s):
                    c.start()

            wgb[...] = wgf[s].astype(BF16)
            wub[...] = wuf[s].astype(BF16)
            wdb[...] = wdf[s].astype(BF16)

        x = xs_ref[...]
        hg = jnp.dot(x, wgb[...], preferred_element_type=F32)
        hu = jnp.dot(x, wub[...], preferred_element_type=F32)
        act = (_silu(hg) * hu).astype(BF16)
        ys_ref[...] = jnp.dot(act, wdb[...], preferred_element_type=F32).astype(ys_ref.dtype)

    @pl.when(jnp.logical_not(used))
    def _():
        ys_ref[...] = jnp.zeros(ys_ref.shape, ys_ref.dtype)


def expert_ffn(xs, plan, w_gate, w_up, w_down, layer):
    n_rows, d = xs.shape
    tm = MOE_ROW_TILE
    de = w_gate.shape[-1]

    grid_spec = pltpu.PrefetchScalarGridSpec(
        num_scalar_prefetch=5, grid=(n_rows // tm,),
        in_specs=[pl.BlockSpec((tm, d), lambda i, te, nu, *_: (jnp.minimum(i, nu[0] - 1), 0)),
                  pl.BlockSpec(memory_space=pl.ANY), pl.BlockSpec(memory_space=pl.ANY),
                  pl.BlockSpec(memory_space=pl.ANY)],
        out_specs=pl.BlockSpec((tm, d), lambda i, *_: (i, 0)),
        scratch_shapes=[pltpu.VMEM((2, d, de), F32), pltpu.VMEM((2, d, de), F32), pltpu.VMEM((2, de, d), F32),
                        pltpu.VMEM((d, de), BF16), pltpu.VMEM((d, de), BF16), pltpu.VMEM((de, d), BF16),
                        pltpu.SemaphoreType.DMA((2,))],
    )
    return pl.pallas_call(
        functools.partial(_expert_kernel, layer=layer), grid_spec=grid_spec,
        out_shape=jax.ShapeDtypeStruct((n_rows, d), BF16),
        compiler_params=_cparams(("arbitrary",)), name="expert_ffn",
    )(plan["tile_expert"], plan["n_used"], plan["tile_first"], plan["tile_next"], plan["tile_slot"],
      xs, w_gate, w_up, w_down)


def _combine_kernel(tot_ref, gran_ref,
                    ys_hbm, src_ref, wt_ref, h_ref, g_ref, *rest, n_out, emit_sum, head_blocks):
    outs, (buf, sem) = rest[:-2], rest[-2:]
    b = pl.program_id(0)
    nb = pl.num_programs(0)
    slot = b % 2

    def copy(s, l, g, size):
        return pltpu.make_async_copy(ys_hbm.at[pl.ds(g, size)], buf.at[s, pl.ds(l, size)], sem.at[s])

    def fetch(blk, s):
        _for_granules(tot_ref, gran_ref, blk, lambda l, g: copy(s, l, g, RUN_ALIGN).start())

    @pl.when(b == 0)
    def _():
        buf[...] = jnp.zeros(buf.shape, buf.dtype)
        fetch(0, 0)

    @pl.when(b + 1 < nb)
    def _():
        fetch(b + 1, 1 - slot)

    _wait_rows(tot_ref, b, lambda size: copy(slot, 0, 0, size).wait())
    y = buf[slot]
    tok2 = 2 * lax.broadcasted_iota(jnp.int32, (TOKEN_TILE, LOCAL_ROWS), 0)
    src = src_ref[...]
    h = h_ref[...]
    for k in range(2):
        pick = jnp.where(src == tok2 + k, 1.0, 0.0).astype(BF16)
        h = h + wt_ref[:, k:k + 1] * jnp.dot(pick, y, preferred_element_type=F32)
    if emit_sum:
        outs[0][...] = h
        outs = outs[1:]
    xn = h * lax.rsqrt(jnp.mean(h * h, axis=-1, keepdims=True) + RMS_EPS)
    if head_blocks is not None:
        val = xn * g_ref[0:1, :]

        @pl.when(b < head_blocks)
        def _():
            outs[0][...] = val

        @pl.when(b >= head_blocks)
        def _():
            outs[1][...] = val
        return
    for k in range(n_out):
        outs[k][...] = (xn * g_ref[k:k + 1, :]).astype(outs[k].dtype)


def moe_combine_norm(ys, plan, wts, h, gains, out_dtype, emit_sum, head_blocks=None):
    tp, d = h.shape
    nb = tp // TOKEN_TILE
    n_out = gains.shape[0]
    tok_spec = pl.BlockSpec((TOKEN_TILE, d), lambda b, *_: (b, 0))
    n_res = n_out + (1 if emit_sum else 0)
    if head_blocks is not None:
        assert n_out == 1 and not emit_sum and head_blocks == nb - 1
        out_specs = [pl.BlockSpec((TOKEN_TILE, d), lambda b, *_: (jnp.minimum(b, head_blocks - 1), 0)),
                     pl.BlockSpec((TOKEN_TILE, d), lambda b, *_: (0, 0))]
        out_shape = [jax.ShapeDtypeStruct((head_blocks * TOKEN_TILE, d), out_dtype),
                     jax.ShapeDtypeStruct((TOKEN_TILE, d), out_dtype)]
    else:
        out_specs = [tok_spec] * n_res
        out_shape = ([jax.ShapeDtypeStruct((tp, d), F32)] if emit_sum else []) + [
            jax.ShapeDtypeStruct((tp, d), out_dtype)] * n_out
    grid_spec = pltpu.PrefetchScalarGridSpec(
        num_scalar_prefetch=2, grid=(nb,),
        in_specs=[pl.BlockSpec(memory_space=pl.ANY),
                  pl.BlockSpec((None, 1, LOCAL_ROWS), lambda b, *_: (b, 0, 0)),
                  pl.BlockSpec((TOKEN_TILE, 2), lambda b, *_: (b, 0)),
                  tok_spec,
                  pl.BlockSpec((n_out, d), lambda b, *_: (0, 0))],
        out_specs=out_specs,
        scratch_shapes=[pltpu.VMEM((2, LOCAL_ROWS, d), BF16), pltpu.SemaphoreType.DMA((2,))],
    )
    return pl.pallas_call(
        functools.partial(_combine_kernel, n_out=n_out, emit_sum=emit_sum, head_blocks=head_blocks),
        grid_spec=grid_spec,
        out_shape=out_shape, compiler_params=_cparams(("arbitrary",)), name="moe_combine_norm",
    )(plan["tot"], plan["gran"], ys, plan["src_row"], wts, h, gains)


def moe_plan(idx, tp):
    i32 = jnp.int32
    tm = MOE_ROW_TILE
    nb = tp // TOKEN_TILE
    n_asg = 2 * TOKEN_TILE
    n_tiles = -(-(2 * tp + nb * N_EXPERTS * (RUN_ALIGN - 1) + N_EXPERTS * (tm - 1)) // tm)
    e_blk = idx.reshape(nb, n_asg)
    onehot = (e_blk[:, :, None] == jnp.arange(N_EXPERTS, dtype=i32)).astype(i32)
    tril = (jnp.arange(n_asg)[:, None] >= jnp.arange(n_asg)[None, :]).astype(F32)
    csum = jnp.einsum("ij,bje->bie", tril, onehot.astype(F32)).astype(i32)
    rank = jnp.sum(onehot * csum, axis=2) - 1
    plen = (csum[:, -1, :] + RUN_ALIGN - 1) // RUN_ALIGN * RUN_ALIGN
    loff = jnp.cumsum(plen, axis=1) - plen
    lpos = jnp.sum(onehot * loff[:, None, :], axis=2) + rank
    hit = lpos[:, :, None] == jnp.arange(LOCAL_ROWS, dtype=i32)
    src = jnp.sum(jnp.where(hit, jnp.arange(1, n_asg + 1, dtype=i32)[None, :, None], 0), axis=1) - 1
    seg = jnp.sum(plen, axis=0)
    seg_pad = (seg + tm - 1) // tm * tm
    gend = jnp.cumsum(seg_pad)
    gstart = gend - seg_pad
    gdst = gstart[None, :] + jnp.cumsum(plen, axis=0) - plen
    g_row = jnp.arange(GRANULES, dtype=i32) * RUN_ALIGN
    run_of = jnp.sum(((loff + plen)[:, None, :] <= g_row[None, :, None]).astype(i32), axis=2)
    run_hot = (run_of[:, :, None] == jnp.arange(N_EXPERTS, dtype=i32)).astype(i32)
    gran = jnp.sum(run_hot * (gdst - loff)[:, None, :], axis=2) + g_row[None, :]
    tot = jnp.sum(plen, axis=1)
    n_used = gend[-1] // tm
    t_row = jnp.minimum(jnp.arange(n_tiles, dtype=i32), n_used - 1) * tm
    tile_expert = jnp.minimum(jnp.sum(gend[None, :] <= t_row[:, None], axis=1), N_EXPERTS - 1)
    ids = jnp.arange(N_EXPERTS, dtype=i32)
    has = seg > 0
    later = jnp.logical_and(ids[None, :] > ids[:, None], has[None, :])
    nxt_e = jnp.min(jnp.where(later, ids[None, :], N_EXPERTS), axis=1)
    nxt_e = jnp.where(nxt_e < N_EXPERTS, nxt_e, -1)
    slot_e = (jnp.cumsum(has.astype(i32)) - 1) % 2
    te_hot = (tile_expert[:, None] == ids[None, :]).astype(i32)
    tile_first = jnp.concatenate([jnp.ones((1,), i32), (tile_expert[1:] != tile_expert[:-1]).astype(i32)])
    plan_tiles = {"tile_first": tile_first, "tile_next": jnp.sum(te_hot * nxt_e[None, :], axis=1),
                  "tile_slot": jnp.sum(te_hot * slot_e[None, :], axis=1)}
    return {
        **{k: v.astype(i32) for k, v in plan_tiles.items()},
        "tot": tot.astype(i32), "gran": gran.reshape(-1).astype(i32),
        "fill0": jnp.concatenate([gstart + seg, gend[-1:]]).astype(i32),
        "filln": jnp.concatenate([seg_pad - seg, n_tiles * tm - gend[-1:]]).astype(i32),
        "src_col": src[:, :, None].astype(i32), "src_row": src[:, None, :].astype(i32),
        "tile_expert": tile_expert.astype(i32), "n_used": n_used.reshape(1).astype(i32),
        "n_rows": n_tiles * tm,
    }


def kernel(x_prompt, x_sample, state_conv, state_ssm, cache_k, cache_v, a_norm, a_w_in, a_conv_w, a_conv_b, a_dt_bias, a_a_log, a_d_skip, a_gate_norm, a_w_out, kv_norm, w_kv, b_norm, b_w_q, b_w_o, moe_norm, router_group_w, router_group_b, router_expert_w, router_expert_b, expert_w_gate, expert_w_up, expert_w_down, final_norm):
    n_pb, seq, d_model = x_prompt.shape
    n_sb, n_new, _ = x_sample.shape
    n_a = a_w_in.shape[0]
    depth = moe_norm.shape[0]
    d_inner = a_w_out.shape[1]
    conv_dim = a_conv_w.shape[2]
    gn = (conv_dim - d_inner) // 2
    n_heads = d_inner // SSM_HEAD_DIM
    d_att = w_kv.shape[1] // 2
    n_p, n_s = n_pb * seq, n_sb * n_new
    lcm = MM_ROW_TILE * TOKEN_TILE // math.gcd(MM_ROW_TILE, TOKEN_TILE)
    tp = -(-(n_p + n_s) // lcm) * lcm
    L = SSD_CHUNK
    assert n_heads <= V7X_LANES and n_new <= L and n_new >= CONV_W - 1 and seq % (max(DILATIONS) * ATTN_Q_BLOCK) == 0

    def pad_rows(parts):
        rows = sum(p.shape[0] for p in parts)
        return jnp.concatenate(parts + [jnp.zeros((tp - rows,) + parts[0].shape[1:], parts[0].dtype)], axis=0)

    def lanes(v):
        return jnp.pad(v.astype(F32), ((0, 0), (0, V7X_LANES - n_heads)))[:, None, :]

    head_of_col = jnp.arange(d_inner, dtype=jnp.int32) // SSM_HEAD_DIM
    ssd_prm = {
        "conv_w": a_conv_w, "conv_b": a_conv_b[:, None, :],
        "dt_bias": lanes(a_dt_bias), "a_log": lanes(a_a_log),
        "d_skip": jnp.repeat(a_d_skip.astype(F32), SSM_HEAD_DIM, axis=1)[:, None, :],
        "gate_norm": a_gate_norm[:, None, :],
        "expand": (jnp.arange(V7X_LANES, dtype=jnp.int32)[:, None] == head_of_col[None, :]).astype(BF16),
    }
    w_dt = jnp.pad(a_w_in[:, :, 2 * d_inner + 2 * gn:], ((0, 0), (0, 0), (0, V7X_LANES - n_heads)))
    w_in_nk = jnp.swapaxes(a_w_in, 1, 2)
    r_w = jnp.concatenate([router_group_w, router_expert_w.reshape(depth, d_model, N_EXPERTS)], axis=2)
    r_w = jnp.pad(r_w, ((0, 0), (0, 0), (0, V7X_LANES - r_w.shape[2])))
    r_b = jnp.concatenate([router_group_b, router_expert_b.reshape(depth, N_EXPERTS)], axis=1)
    r_b = jnp.pad(r_b, ((0, 0), (0, V7X_LANES - r_b.shape[1])))[:, None, :]

    h = pad_rows([x_prompt.reshape(n_p, d_model), x_sample.reshape(n_s, d_model)])
    (xn,) = add_norm(h, [], a_norm[0:1], BF16, emit_sum=False)
    conv_p, ssm_p, conv_s, ssm_s = [], [], [], []
    kv = None
    for layer in range(depth):
        if layer < n_a:
            zxbc = matmul(xn, w_in_nk, layer, 2 * d_inner + 2 * gn, 1024, w_is_nk=True)
            dt_raw = matmul(xn, w_dt, layer, V7X_LANES, V7X_LANES)
            y_p, cp, sp = ssd_mixer(zxbc, dt_raw, 0, n_pb, seq // L, L, None, None, ssd_prm, layer, d_inner, gn)
            precise = layer == 0
            if precise:
                x_s = x_sample.reshape(n_s, d_model)
                (xn_s,) = add_norm(x_s, [], a_norm[0:1], F32, emit_sum=False)
                zx_s = matmul_f32(xn_s, w_in_nk, layer, 2 * d_inner + 2 * gn, 1024, w_is_nk=True)
                dt_s = matmul_f32(xn_s, w_dt, layer, V7X_LANES, V7X_LANES)
            else:
                zx_s, dt_s = zxbc[n_p:n_p + n_s], dt_raw[n_p:n_p + n_s]
            zs = jnp.pad(zx_s.reshape(n_sb, n_new, -1), ((0, 0), (0, L - n_new), (0, 0)))
            ds = jnp.pad(dt_s.reshape(n_sb, n_new, -1), ((0, 0), (0, L - n_new), (0, 0)))
            y_s, cs, ss = ssd_mixer(zs.reshape(n_sb * L, -1), ds.reshape(n_sb * L, -1), 0, n_sb, 1, n_new,
                                    state_conv[layer], state_ssm[layer].reshape(n_sb, n_heads * SSM_HEAD_DIM, D_STATE),
                                    ssd_prm, layer, d_inner, gn, precise=precise)
            conv_p.append(cp); ssm_p.append(sp); conv_s.append(cs); ssm_s.append(ss)
            y_s = y_s.reshape(n_sb, L, d_inner)[:, :n_new].reshape(n_s, d_inner)
            y = pad_rows([y_p, y_s.astype(BF16)])
            h = matmul(y, a_w_out, layer, d_model, 512, res=h)
            if precise:
                h_s = matmul_f32(y_s, a_w_out, layer, d_model, 512, res=x_s)
                h = lax.dynamic_update_slice(h, h_s, (n_p, 0))
        else:
            b = layer - n_a
            q = matmul(xn, b_w_q, b, b_w_q.shape[2], 1024)
            o_p = attn_prompt(q, kv, n_pb, seq)
            qs = q[n_p:n_p + n_s].reshape(n_sb, n_new, len(DILATIONS), N_KV_HEADS, ATT_HEAD_DIM)
            qs = qs.transpose(0, 3, 2, 1, 4).reshape(n_sb, N_KV_HEADS, len(DILATIONS) * n_new, ATT_HEAD_DIM)
            qs = jnp.pad(qs, ((0, 0), (0, 0), (0, 16 - len(DILATIONS) * n_new), (0, 0)))
            kvs = kv[n_p:n_p + n_s].reshape(n_sb, n_new, 2, N_KV_HEADS, ATT_HEAD_DIM).transpose(2, 0, 3, 1, 4)
            kvs = jnp.pad(kvs, ((0, 0), (0, 0), (0, 0), (0, V7X_LANES - n_new), (0, 0)))
            o_s = attn_sample(qs, kvs[0], kvs[1], cache_k.reshape(n_sb, -1, ATT_HEAD_DIM),
                              cache_v.reshape(n_sb, -1, ATT_HEAD_DIM), n_new)
            o = pad_rows([o_p, o_s.transpose(0, 2, 1, 3).reshape(n_s, d_att)])
            h = matmul(o, b_w_o, b, d_model, 1024, res=h)
        xm, idx, wts = router(h, moe_norm[layer:layer + 1], r_w[layer], r_b[layer])
        plan = moe_plan(idx, tp)
        ys = expert_ffn(moe_dispatch(xm, plan), plan, expert_w_gate, expert_w_up, expert_w_down, layer)
        if layer + 1 < n_a:
            h, xn = moe_combine_norm(ys, plan, wts, h, a_norm[layer + 1:layer + 2], BF16, emit_sum=True)
        elif layer + 1 == n_a:
            h, xkv, xn = moe_combine_norm(ys, plan, wts, h, jnp.stack([kv_norm, b_norm[0]]), BF16, emit_sum=True)
            kv = matmul(xkv, w_kv, None, w_kv.shape[1], 1024)
        elif layer + 1 < depth:
            h, xn = moe_combine_norm(ys, plan, wts, h, b_norm[layer + 1 - n_a:layer + 2 - n_a], BF16,
                                     emit_sum=True)
        else:
            y_head, y_tail = moe_combine_norm(ys, plan, wts, h, final_norm[None, :], F32, emit_sum=False,
                                              head_blocks=n_p // TOKEN_TILE)

    y_prompt = y_head.reshape(n_pb, seq, d_model)
    y_sample = y_tail[:n_s].reshape(n_sb, n_new, d_model)
    keep = min(max(DILATIONS) * (N_DIL_KEYS - 1), seq)
    kv_p = kv[:n_p].reshape(n_pb, seq, 2, N_KV_HEADS, ATT_HEAD_DIM)[:, seq - keep:]
    kv_s = kv[n_p:n_p + n_s].reshape(n_sb, n_new, 2, N_KV_HEADS, ATT_HEAD_DIM)
    ssm_shape = (n_heads, SSM_HEAD_DIM, D_STATE)
    return (y_prompt, y_sample,
            jnp.stack(conv_p), jnp.stack(ssm_p).reshape((n_a, n_pb) + ssm_shape),
            kv_p[:, :, 0], kv_p[:, :, 1],
            jnp.stack(conv_s), jnp.stack(ssm_s).reshape((n_a, n_sb) + ssm_shape),
            kv_s[:, :, 0], kv_s[:, :, 1])
```

```python
import functools
import math

import jax
import jax.numpy as jnp
from jax import lax
from jax.experimental import pallas as pl
from jax.experimental.pallas import tpu as pltpu

F32 = jnp.float32
BF16 = jnp.bfloat16

V7X_VMEM_BYTES = 64 * 1024 * 1024
V7X_LANES = 128
V7X_SUBLANES = 8
VMEM_LIMIT = V7X_VMEM_BYTES * 7 // 8

RMS_EPS = 1e-6
SSM_HEAD_DIM = 64
N_SSM_GROUPS = 8
D_STATE = 128
CONV_W = 4
SSD_CHUNK = 128
ATT_HEAD_DIM = 128
N_KV_HEADS = 8
DILATIONS = (1, 4, 16)
N_DIL_KEYS = 129
assert all(d & (d - 1) == 0 for d in DILATIONS)
N_EXPERT_GROUPS = 4
EXPERTS_PER_GROUP = 8
N_EXPERTS = N_EXPERT_GROUPS * EXPERTS_PER_GROUP
NEG_INF = -1e30

TOKEN_TILE = 256
MM_ROW_TILE = 768
MOE_ROW_TILE = 256
ATTN_Q_BLOCK = 256
ATTN_UNROLL = 4
ATTN_SAMPLE_CHUNK = 2048


def _cparams(sem):
    return pltpu.CompilerParams(dimension_semantics=sem, vmem_limit_bytes=VMEM_LIMIT)


def _norm_kernel(*refs, n_add, n_out, emit_sum):
    h = refs[0][...]
    for a in refs[1:1 + n_add]:
        h = h + a[...]
    g_ref = refs[1 + n_add]
    outs = refs[2 + n_add:]
    if emit_sum:
        outs[0][...] = h
        outs = outs[1:]
    xn = h * lax.rsqrt(jnp.mean(h * h, axis=-1, keepdims=True) + RMS_EPS)
    for k in range(n_out):
        outs[k][...] = (xn * g_ref[k:k + 1, :]).astype(outs[k].dtype)


def add_norm(h, adds, gains, out_dtype, emit_sum):
    tp, d = h.shape
    n_out = gains.shape[0]
    tm = min(TOKEN_TILE, tp)
    in_specs = [pl.BlockSpec((tm, d), lambda i: (i, 0))]
    args = [h]
    for arr, off in adds:
        in_specs.append(pl.BlockSpec((tm, d), functools.partial(lambda i, o: (i + o, 0), o=off)))
        args.append(arr)
    in_specs.append(pl.BlockSpec((n_out, d), lambda i: (0, 0)))
    args.append(gains)
    out_shape, out_specs = [], []
    if emit_sum:
        out_shape.append(jax.ShapeDtypeStruct((tp, d), F32))
        out_specs.append(pl.BlockSpec((tm, d), lambda i: (i, 0)))
    for _ in range(n_out):
        out_shape.append(jax.ShapeDtypeStruct((tp, d), out_dtype))
        out_specs.append(pl.BlockSpec((tm, d), lambda i: (i, 0)))
    return pl.pallas_call(
        functools.partial(_norm_kernel, n_add=len(adds), n_out=n_out, emit_sum=emit_sum),
        grid=(tp // tm,), in_specs=in_specs, out_specs=out_specs, out_shape=out_shape,
        compiler_params=_cparams(("parallel",)), name="add_norm",
    )(*args)


def _mm_kernel(*refs, has_res, w_is_nk):
    if has_res:
        x_ref, w_ref, r_ref, o_ref, wb_ref = refs
    else:
        x_ref, w_ref, o_ref, wb_ref = refs

    @pl.when(pl.program_id(1) == 0)
    def _():
        wb_ref[...] = w_ref[...].astype(BF16)

    contract = (((1,), (1 if w_is_nk else 0,)), ((), ()))
    acc = lax.dot_general(x_ref[...].astype(BF16), wb_ref[...], contract, preferred_element_type=F32)
    if has_res:
        acc = r_ref[...] + acc
    o_ref[...] = acc


def matmul(x, w, layer, n_cols, tn, res=None, tm=MM_ROW_TILE, w_is_nk=False):
    tp, k = x.shape
    assert tp % tm == 0 and n_cols % tn == 0
    if w_is_nk:
        w_spec = pl.BlockSpec((None, tn, k), lambda j, i: (layer, j, 0))
    elif layer is None:
        w_spec = pl.BlockSpec((k, tn), lambda j, i: (0, j))
    else:
        w_spec = pl.BlockSpec((None, k, tn), lambda j, i: (layer, 0, j))
    in_specs = [pl.BlockSpec((tm, k), lambda j, i: (i, 0)), w_spec]
    args = [x, w]
    if res is not None:
        in_specs.append(pl.BlockSpec((tm, tn), lambda j, i: (i, j)))
        args.append(res)
    return pl.pallas_call(
        functools.partial(_mm_kernel, has_res=res is not None, w_is_nk=w_is_nk),
        grid=(n_cols // tn, tp // tm), in_specs=in_specs,
        out_specs=pl.BlockSpec((tm, tn), lambda j, i: (i, j)),
        out_shape=jax.ShapeDtypeStruct((tp, n_cols), F32),
        scratch_shapes=[pltpu.VMEM((tn, k) if w_is_nk else (k, tn), BF16)],
        compiler_params=_cparams(("arbitrary", "arbitrary")), name="proj",
    )(*args)


def _mm_f32_kernel(*refs, has_res, w_is_nk):
    if has_res:
        x_ref, w_ref, r_ref, o_ref = refs
    else:
        x_ref, w_ref, o_ref = refs
    contract = (((1,), (1 if w_is_nk else 0,)), ((), ()))
    acc = lax.dot_general(x_ref[...], w_ref[...], contract, preferred_element_type=F32,
                          precision=lax.Precision.HIGHEST)
    if has_res:
        acc = r_ref[...] + acc
    o_ref[...] = acc


def matmul_f32(x, w, layer, n_cols, tn, res=None, w_is_nk=False):
    m, k = x.shape
    if w_is_nk:
        w_spec = pl.BlockSpec((None, tn, k), lambda j: (layer, j, 0))
    else:
        w_spec = pl.BlockSpec((None, k, tn), lambda j: (layer, 0, j))
    in_specs = [pl.BlockSpec((m, k), lambda j: (0, 0)), w_spec]
    args = [x, w]
    if res is not None:
        in_specs.append(pl.BlockSpec((m, tn), lambda j: (0, j)))
        args.append(res)
    return pl.pallas_call(
        functools.partial(_mm_f32_kernel, has_res=res is not None, w_is_nk=w_is_nk),
        grid=(n_cols // tn,), in_specs=in_specs, out_specs=pl.BlockSpec((m, tn), lambda j: (0, j)),
        out_shape=jax.ShapeDtypeStruct((m, n_cols), F32),
        compiler_params=_cparams(("parallel",)), name="proj_f32",
    )(*args)


def _silu(x):
    return x * (0.5 * jnp.tanh(0.5 * x) + 0.5)


def _split2(v):
    hi = v.astype(BF16)
    lo = (v - hi.astype(F32)).astype(BF16)
    return hi, lo


def _ssd_kernel(z_ref, x_ref, b_ref, c_ref, dt_ref, conv0_ref, ssm0_ref,
                cw_ref, cb_ref, dtb_ref, alog_ref, dsk_ref, gn_ref, e_ref,
                y_ref, convo_ref, ssmo_ref,
                xpad, xs_s, xlo_s, xhi_s, xw_s, bm_s, bmt_s, cm_s, y_s, ea_s, state_s, dec_s,
                *, n_valid, has_init):
    L = SSD_CHUNK
    mx = xlo_s.dtype
    prec = lax.Precision.HIGHEST if mx == F32 else None
    d_inner = xs_s.shape[1]
    gn = bm_s.shape[1]
    n_heads = d_inner // SSM_HEAD_DIM
    pair = 2 * SSM_HEAD_DIM
    gcols = d_inner // N_SSM_GROUPS
    c = pl.program_id(1)
    pad0 = V7X_SUBLANES
    keep = CONV_W - 1

    @pl.when(c == 0)
    def _():
        if has_init:
            xpad[pad0 - keep:pad0, :] = conv0_ref[...]
            for g in range(N_SSM_GROUPS):
                state_s[:, g * gcols:(g + 1) * gcols] = ssm0_ref[g * gcols:(g + 1) * gcols, :].T
        else:
            xpad[0:pad0, :] = jnp.zeros((pad0, xpad.shape[1]), F32)
            state_s[...] = jnp.zeros(state_s.shape, F32)

    xpad[pad0:pad0 + L, 0:d_inner] = x_ref[...]
    xpad[pad0:pad0 + L, d_inner:d_inner + gn] = b_ref[...]
    xpad[pad0:pad0 + L, d_inner + gn:d_inner + 2 * gn] = c_ref[...]
    strip = D_STATE
    lane = lax.broadcasted_iota(jnp.int32, (L, strip), 1)
    even_head = (lane // SSM_HEAD_DIM) % 2 == 0
    for s in range(xpad.shape[1] // strip):
        cs = slice(s * strip, (s + 1) * strip)
        rows = xpad[:, cs]
        acc = cb_ref[:, cs] + cw_ref[keep:keep + 1, cs] * rows[pad0:pad0 + L]
        for k in range(keep):
            acc = acc + cw_ref[k:k + 1, cs] * pltpu.roll(rows, keep - k, 0)[pad0:pad0 + L]
        v = _silu(acc)
        if s * strip < d_inner:
            xs_s[:, cs] = v
            vb = v.astype(mx)
            xlo_s[:, cs] = jnp.where(even_head, vb, jnp.zeros_like(vb))
            xhi_s[:, cs] = jnp.where(even_head, jnp.zeros_like(vb), vb)
        elif s * strip < d_inner + gn:
            o = s * strip - d_inner
            bm_s[:, o:o + strip] = v.astype(mx)
            for q in range(strip // D_STATE):
                gi = o // D_STATE + q
                bmt_s[:, gi * L:(gi + 1) * L] = v[:, q * D_STATE:(q + 1) * D_STATE].T.astype(mx)
        else:
            o = s * strip - d_inner - gn
            cm_s[:, o:o + strip] = v.astype(mx)

    tail = xpad[pad0 + n_valid - keep:pad0 + n_valid, :]
    xpad[pad0 - keep:pad0, :] = tail

    @pl.when(c == pl.num_programs(1) - 1)
    def _():
        convo_ref[...] = tail

    row = lax.broadcasted_iota(jnp.int32, (L, V7X_LANES), 0)
    dtv = dt_ref[...] + dtb_ref[...]
    dt = jnp.maximum(dtv, 0.0) + jnp.log(1.0 + jnp.exp(-jnp.abs(dtv)))
    dt = jnp.where(row < n_valid, dt, 0.0)
    a = -jnp.exp(alog_ref[...])
    adt = dt * a
    ii = lax.broadcasted_iota(jnp.int32, (L, L), 0)
    jj = lax.broadcasted_iota(jnp.int32, (L, L), 1)
    causal = ii >= jj
    tri = jnp.where(causal, 1.0, 0.0).astype(F32)
    a_cs = jnp.dot(tri, adt, preferred_element_type=F32, precision=lax.Precision.HIGHEST)
    dec_s[0] = a_cs
    dec_s[1] = a_cs.T
    dec_s[2] = dt.T
    dec_s[3] = jnp.where(causal, 0.0, NEG_INF).astype(F32)
    ea = jnp.exp(a_cs)
    dte = dt * jnp.exp(a_cs[L - 1:L, :] - a_cs)
    ea_p, dte_p = _split2(ea), _split2(dte)
    ex_strip = 256
    for s in range(d_inner // ex_strip):
        cs = slice(s * ex_strip, (s + 1) * ex_strip)

        def expand(v, pieces):
            if mx == F32:
                return jnp.dot(v, e_ref[:, cs].astype(F32), preferred_element_type=F32, precision=prec)
            return (jnp.dot(pieces[0], e_ref[:, cs], preferred_element_type=F32)
                    + jnp.dot(pieces[1], e_ref[:, cs], preferred_element_type=F32))

        ea_s[:, cs] = expand(ea, ea_p)
        xw_s[:, cs] = (xs_s[:, cs] * expand(dte, dte_p)).astype(mx)

    for g in range(N_SSM_GROUPS):
        ns = slice(g * D_STATE, (g + 1) * D_STATE)
        cg = cm_s[:, ns]
        cb = lax.dot_general(cg, bm_s[:, ns], (((1,), (1,)), ((), ())), preferred_element_type=F32,
                             precision=prec)
        for q in range(gcols // pair):
            col = slice(g * gcols + q * pair, g * gcols + (q + 1) * pair)
            acc = None
            for hh, xsrc in ((0, xlo_s), (1, xhi_s)):
                h = (g * gcols + q * pair) // SSM_HEAD_DIM + hh
                seg = dec_s[0, :, h:h + 1] - dec_s[1, h:h + 1, :]
                m = (cb * jnp.exp(seg + dec_s[3]) * dec_s[2, h:h + 1, :]).astype(mx)
                part = jnp.dot(m, xsrc[:, col], preferred_element_type=F32, precision=prec)
                acc = part if acc is None else acc + part
            st = state_s[:, col]
            y_off = jnp.dot(cg, st.astype(mx), preferred_element_type=F32, precision=prec)
            y_s[:, col] = acc + y_off * ea_s[:, col] + dsk_ref[:, col] * xs_s[:, col]
            state_s[:, col] = st * ea_s[L - 1:L, col] + jnp.dot(
                bmt_s[:, g * L:(g + 1) * L], xw_s[:, col], preferred_element_type=F32, precision=prec)

    for g in range(N_SSM_GROUPS):
        gs = slice(g * gcols, (g + 1) * gcols)
        v = y_s[:, gs] * _silu(z_ref[:, gs])
        v = v * lax.rsqrt(jnp.mean(v * v, axis=-1, keepdims=True) + RMS_EPS)
        y_ref[:, gs] = (v * gn_ref[:, gs]).astype(y_ref.dtype)

    @pl.when(c == pl.num_programs(1) - 1)
    def _():
        for g in range(N_SSM_GROUPS):
            ssmo_ref[g * gcols:(g + 1) * gcols, :] = state_s[:, g * gcols:(g + 1) * gcols].T


def ssd_mixer(zxbc, dt_raw, row0, n_seq, n_chunks, n_valid, conv0, ssm0, prm, layer, d_inner, gn,
              precise=False):
    L = SSD_CHUNK
    assert row0 % L == 0
    rb = row0 // L
    mx = F32 if precise else BF16
    conv_dim = d_inner + 2 * gn
    has_init = conv0 is not None
    if not has_init:
        conv0 = jnp.zeros((n_seq, CONV_W - 1, conv_dim), F32)
        ssm0 = jnp.zeros((n_seq, 8, D_STATE), F32)
    n_heads = d_inner // SSM_HEAD_DIM
    rows = n_seq * n_chunks * L

    def rmap(b, c):
        return rb + b * n_chunks + c

    zb = d_inner // gn
    in_specs = [
        pl.BlockSpec((L, d_inner), lambda b, c: (rmap(b, c), 0)),
        pl.BlockSpec((L, d_inner), lambda b, c: (rmap(b, c), 1)),
        pl.BlockSpec((L, gn), lambda b, c: (rmap(b, c), 2 * zb)),
        pl.BlockSpec((L, gn), lambda b, c: (rmap(b, c), 2 * zb + 1)),
        pl.BlockSpec((L, V7X_LANES), lambda b, c: (rmap(b, c), 0)),
        pl.BlockSpec((None, CONV_W - 1, conv_dim), lambda b, c: (b, 0, 0)),
        pl.BlockSpec((None,) + ssm0.shape[1:], lambda b, c: (b, 0, 0)),
        pl.BlockSpec((None, CONV_W, conv_dim), lambda b, c: (layer, 0, 0)),
        pl.BlockSpec((None, 1, conv_dim), lambda b, c: (layer, 0, 0)),
        pl.BlockSpec((None, 1, V7X_LANES), lambda b, c: (layer, 0, 0)),
        pl.BlockSpec((None, 1, V7X_LANES), lambda b, c: (layer, 0, 0)),
        pl.BlockSpec((None, 1, d_inner), lambda b, c: (layer, 0, 0)),
        pl.BlockSpec((None, 1, d_inner), lambda b, c: (layer, 0, 0)),
        pl.BlockSpec((V7X_LANES, d_inner), lambda b, c: (0, 0)),
    ]
    out_shape = [
        jax.ShapeDtypeStruct((rows, d_inner), mx),
        jax.ShapeDtypeStruct((n_seq, CONV_W - 1, conv_dim), F32),
        jax.ShapeDtypeStruct((n_seq, n_heads * SSM_HEAD_DIM, D_STATE), F32),
    ]
    out_specs = [
        pl.BlockSpec((L, d_inner), lambda b, c: (b * n_chunks + c, 0)),
        pl.BlockSpec((None, CONV_W - 1, conv_dim), lambda b, c: (b, 0, 0)),
        pl.BlockSpec((None, n_heads * SSM_HEAD_DIM, D_STATE), lambda b, c: (b, 0, 0)),
    ]
    scratch = [
        pltpu.VMEM((L + V7X_SUBLANES, conv_dim), F32),
        pltpu.VMEM((L, d_inner), F32),
        pltpu.VMEM((L, d_inner), mx),
        pltpu.VMEM((L, d_inner), mx),
        pltpu.VMEM((L, d_inner), mx),
        pltpu.VMEM((L, gn), mx),
        pltpu.VMEM((D_STATE, N_SSM_GROUPS * L), mx),
        pltpu.VMEM((L, gn), mx),
        pltpu.VMEM((L, d_inner), F32),
        pltpu.VMEM((L, d_inner), F32),
        pltpu.VMEM((D_STATE, d_inner), F32),
        pltpu.VMEM((4, L, V7X_LANES), F32),
    ]
    return pl.pallas_call(
        functools.partial(_ssd_kernel, n_valid=n_valid, has_init=has_init),
        grid=(n_seq, n_chunks), in_specs=in_specs, out_specs=out_specs, out_shape=out_shape,
        scratch_shapes=scratch,
        compiler_params=_cparams(("arbitrary", "arbitrary")), name="ssd_mixer",
    )(zxbc, zxbc, zxbc, zxbc, dt_raw, conv0, ssm0,
      prm["conv_w"], prm["conv_b"], prm["dt_bias"], prm["a_log"], prm["d_skip"], prm["gate_norm"],
      prm["expand"])


def _attn_prompt_kernel(q0_ref, q1_ref, q2_ref, k_ref, v_ref, o_ref, *scratch, seq):
    og_s, lse_s = scratch[:len(DILATIONS)], scratch[len(DILATIONS):]
    QB = ATTN_Q_BLOCK
    W = N_DIL_KEYS - 1
    scale = ATT_HEAD_DIM ** -0.5
    dist_c = (lax.broadcasted_iota(jnp.int32, (QB, QB), 0) - lax.broadcasted_iota(jnp.int32, (QB, QB), 1))
    cur_ok = jnp.logical_and(dist_c >= 0, dist_c <= W)
    kq = lax.broadcasted_iota(jnp.int32, (QB, W), 1) - lax.broadcasted_iota(jnp.int32, (QB, W), 0)

    def scores(q, k_rows, ok):
        s = lax.dot_general(q, k_rows.astype(BF16), (((1,), (1,)), ((), ())),
                            preferred_element_type=F32) * scale
        return jnp.where(ok, s, NEG_INF)

    for g, (q_ref, d) in enumerate(zip((q0_ref, q1_ref, q2_ref), DILATIONS)):
        nb_r = seq // d // QB

        def body(idx, carry, g=g, q_ref=q_ref, d=d, nb_r=nb_r):
            r = idx // nb_r
            mb = idx % nb_r
            start = r + d * QB * mb
            rows = pl.ds(start, QB, stride=d) if d > 1 else pl.ds(start, QB)
            q = q_ref[rows, :].astype(BF16)
            s_c = scores(q, k_ref[rows, :], cur_ok)
            m = jnp.max(s_c, axis=-1, keepdims=True)
            if nb_r > 1:
                pstart = jnp.maximum(start - d * W, r)
                prev_min = jnp.where(mb > 0, 0, W)
                prows = pl.ds(pstart, W, stride=d) if d > 1 else pl.ds(pstart, W)
                s_p = scores(q, k_ref[prows, :], kq >= prev_min)
                m = jnp.maximum(m, jnp.max(s_p, axis=-1, keepdims=True))
            p_c = jnp.exp(s_c - m)
            den = jnp.sum(p_c, axis=-1, keepdims=True)
            pv = jnp.dot(p_c.astype(BF16), v_ref[rows, :].astype(BF16), preferred_element_type=F32)
            if nb_r > 1:
                p_p = jnp.exp(s_p - m)
                den = den + jnp.sum(p_p, axis=-1, keepdims=True)
                pv = pv + jnp.dot(p_p.astype(BF16), v_ref[prows, :].astype(BF16), preferred_element_type=F32)
            og_s[g][rows, :] = pv / den
            lse_s[g][rows, :] = jnp.broadcast_to(m + jnp.log(den), (QB, ATT_HEAD_DIM))
            return carry

        n_blk = seq // QB
        lax.fori_loop(0, n_blk, body, 0, unroll=min(ATTN_UNROLL, n_blk))

    QB = 128

    def merge(i, carry):
        rows = pl.ds(pl.multiple_of(i * QB, QB), QB)
        lse = [lse_s[g][rows, :] for g in range(len(DILATIONS))]
        top = functools.reduce(jnp.maximum, lse)
        w = [jnp.exp(x - top) for x in lse]
        num = functools.reduce(lambda a, b: a + b, [w[g] * og_s[g][rows, :] for g in range(len(DILATIONS))])
        o_ref[rows, :] = num / functools.reduce(lambda a, b: a + b, w)
        return carry

    lax.fori_loop(0, seq // QB, merge, 0, unroll=ATTN_UNROLL)


def attn_prompt(q, kv, n_seq, seq):
    hd = ATT_HEAD_DIM
    nh = N_KV_HEADS

    def qspec(g):
        return pl.BlockSpec((seq, hd), lambda b, h: (b, g * nh + h))

    return pl.pallas_call(
        functools.partial(_attn_prompt_kernel, seq=seq),
        grid=(n_seq, nh),
        in_specs=[qspec(0), qspec(1), qspec(2),
                  pl.BlockSpec((seq, hd), lambda b, h: (b, h)),
                  pl.BlockSpec((seq, hd), lambda b, h: (b, nh + h))],
        out_specs=pl.BlockSpec((seq, hd), lambda b, h: (b, h)),
        out_shape=jax.ShapeDtypeStruct((n_seq * seq, nh * hd), F32),
        scratch_shapes=[pltpu.VMEM((seq, hd), F32)] * (2 * len(DILATIONS)),
        compiler_params=_cparams(("parallel", "parallel")), name="attn_prompt",
    )(q, q, q, kv, kv)


def _attn_sample_kernel(q_ref, kn_ref, vn_ref, kc_ref, vc_ref, o_ref, *, n_new, past):
    nh = N_KV_HEADS
    gr = nh * n_new
    QR = q_ref.shape[0]
    scale = ATT_HEAD_DIM ** -0.5
    q = q_ref[...].astype(BF16)
    h_shift, s_mask = nh.bit_length() - 1, n_new - 1

    def visible(rows0, n_keys, pos0, n_pos):
        rr = lax.broadcasted_iota(jnp.int32, (QR, n_keys), 0)
        jj = lax.broadcasted_iota(jnp.int32, (QR, n_keys), 1) + rows0
        g = jnp.right_shift(rr, gr.bit_length() - 1)
        dil = jnp.where(g == 0, DILATIONS[0], jnp.where(g == 1, DILATIONS[1], DILATIONS[2]))
        rel = jnp.right_shift(jj, h_shift)
        dist = past + jnp.bitwise_and(rr, s_mask) - (pos0 + rel)
        ok = jnp.logical_and(dist >= 0, dist <= dil * (N_DIL_KEYS - 1))
        ok = jnp.logical_and(ok, jnp.bitwise_and(dist, dil - 1) == 0)
        row_head = jnp.right_shift(jnp.bitwise_and(rr, gr - 1), n_new.bit_length() - 1)
        ok = jnp.logical_and(ok, row_head == jnp.bitwise_and(jj, nh - 1))
        return jnp.logical_and(ok, jnp.logical_and(g < len(DILATIONS), rel < n_pos))

    def step(carry, k_rows, v_rows, ok):
        m, l, acc = carry
        s = lax.dot_general(q, k_rows.astype(BF16), (((1,), (1,)), ((), ())),
                            preferred_element_type=F32) * scale
        s = jnp.where(ok, s, NEG_INF)
        m_new = jnp.maximum(m, jnp.max(s, axis=-1, keepdims=True))
        alpha = jnp.exp(m - m_new)
        p = jnp.exp(s - m_new)
        l = alpha * l + jnp.sum(p, axis=-1, keepdims=True)
        acc = alpha * acc + jnp.dot(p.astype(BF16), v_rows.astype(BF16), preferred_element_type=F32)
        return m_new, l, acc

    chunk = ATTN_SAMPLE_CHUNK

    def cache_step(c, carry):
        rows = pl.ds(pl.multiple_of(c * chunk, chunk), chunk)
        return step(carry, kc_ref[rows, :], vc_ref[rows, :], visible(c * chunk, chunk, 0, past))

    init = (jnp.full((QR, 1), NEG_INF, F32), jnp.zeros((QR, 1), F32), jnp.zeros((QR, ATT_HEAD_DIM), F32))
    carry = lax.fori_loop(0, past * nh // chunk, cache_step, init)
    m, l, acc = step(carry, kn_ref[...], vn_ref[...], visible(0, kn_ref.shape[0], past, n_new))
    ms = [m[g * gr:(g + 1) * gr] for g in range(len(DILATIONS))]
    m_all = functools.reduce(jnp.maximum, ms)
    num = jnp.zeros((gr, ATT_HEAD_DIM), F32)
    den = jnp.zeros((gr, 1), F32)
    for g in range(len(DILATIONS)):
        w = jnp.exp(ms[g] - m_all)
        num = num + w * acc[g * gr:(g + 1) * gr]
        den = den + w * l[g * gr:(g + 1) * gr]
    o_ref[...] = num / den


def attn_sample(q_s, k_new, v_new, cache_k, cache_v, n_new):
    nb, qr, hd = q_s.shape
    nh = N_KV_HEADS
    rows_c, rows_n = cache_k.shape[1], k_new.shape[1]
    assert nh & (nh - 1) == 0 and n_new & (n_new - 1) == 0 and rows_c % ATTN_SAMPLE_CHUNK == 0

    def per_seq(rows):
        return pl.BlockSpec((None, rows, hd), lambda b: (b, 0, 0))

    return pl.pallas_call(
        functools.partial(_attn_sample_kernel, n_new=n_new, past=rows_c // nh),
        grid=(nb,),
        in_specs=[per_seq(qr), per_seq(rows_n), per_seq(rows_n), per_seq(rows_c), per_seq(rows_c)],
        out_specs=per_seq(nh * n_new),
        out_shape=jax.ShapeDtypeStruct((nb, nh * n_new, hd), F32),
        compiler_params=_cparams(("parallel",)), name="attn_sample",
    )(q_s, k_new, v_new, cache_k, cache_v)


def _router_kernel(h_ref, g_ref, w_ref, b_ref, xn_ref, idx_ref, wt_ref):
    h = h_ref[...]
    xn = h * lax.rsqrt(jnp.mean(h * h, axis=-1, keepdims=True) + RMS_EPS) * g_ref[...]
    xn_ref[...] = xn.astype(xn_ref.dtype)
    x_hi, x_lo = _split2(xn)
    w = w_ref[...]
    w_hi, w_lo = _split2(w)
    logits = (jnp.dot(x_hi, w_hi, preferred_element_type=F32) + jnp.dot(x_lo, w_hi, preferred_element_type=F32)
              + jnp.dot(x_hi, w_lo, preferred_element_type=F32)) + b_ref[...]
    tm = logits.shape[0]
    lane = lax.broadcasted_iota(jnp.int32, logits.shape, 1)
    big = jnp.int32(V7X_LANES)
    is_g = lane < N_EXPERT_GROUPS
    gl = jnp.where(is_g, logits, NEG_INF)
    gmax = jnp.max(gl, axis=-1, keepdims=True)
    g_idx = jnp.min(jnp.where(jnp.logical_and(is_g, gl == gmax), lane, big), axis=-1, keepdims=True)
    g_w = 1.0 / jnp.sum(jnp.where(is_g, jnp.exp(gl - gmax), 0.0), axis=-1, keepdims=True)
    lo = N_EXPERT_GROUPS + g_idx * EXPERTS_PER_GROUP
    in_grp = jnp.logical_and(lane >= lo, lane < lo + EXPERTS_PER_GROUP)
    el = jnp.where(in_grp, logits, NEG_INF)
    v1 = jnp.max(el, axis=-1, keepdims=True)
    i1 = jnp.min(jnp.where(jnp.logical_and(in_grp, el == v1), lane, big), axis=-1, keepdims=True)
    el2 = jnp.where(lane == i1, NEG_INF, el)
    v2 = jnp.max(el2, axis=-1, keepdims=True)
    i2 = jnp.min(jnp.where(jnp.logical_and(in_grp, jnp.logical_and(el2 == v2, lane != i1)), lane, big),
                 axis=-1, keepdims=True)
    e2 = jnp.exp(v2 - v1)
    w1 = g_w / (1.0 + e2)
    w2 = g_w * e2 / (1.0 + e2)
    col = lax.broadcasted_iota(jnp.int32, (tm, 2), 1)
    idx_ref[...] = jnp.where(col == 0, i1, i2) - N_EXPERT_GROUPS
    wt_ref[...] = jnp.where(col == 0, w1, w2)


def router(h, gain, w_cat, b_cat):
    tp, d = h.shape
    tm = TOKEN_TILE
    return pl.pallas_call(
        _router_kernel, grid=(tp // tm,),
        in_specs=[pl.BlockSpec((tm, d), lambda i: (i, 0)),
                  pl.BlockSpec((1, d), lambda i: (0, 0)),
                  pl.BlockSpec((d, V7X_LANES), lambda i: (0, 0)),
                  pl.BlockSpec((1, V7X_LANES), lambda i: (0, 0))],
        out_specs=[pl.BlockSpec((tm, d), lambda i: (i, 0)),
                   pl.BlockSpec((tm, 2), lambda i: (i, 0)),
                   pl.BlockSpec((tm, 2), lambda i: (i, 0))],
        out_shape=[jax.ShapeDtypeStruct((tp, d), BF16),
                   jax.ShapeDtypeStruct((tp, 2), jnp.int32),
                   jax.ShapeDtypeStruct((tp, 2), F32)],
        compiler_params=_cparams(("parallel",)), name="router",
    )(h, gain, w_cat, b_cat)


RUN_ALIGN = 16
LOCAL_ROWS = 2 * TOKEN_TILE + N_EXPERTS * RUN_ALIGN
GRANULES = LOCAL_ROWS // RUN_ALIGN


def _for_pieces(n, chunk, fn):
    n_big = n // chunk
    lax.fori_loop(0, n_big, lambda j, c: (fn(j * chunk, chunk), c)[1], 0)
    done = n_big * chunk
    size = chunk // 2
    while size >= RUN_ALIGN:
        hit = (n & size) != 0
        pl.when(hit)(functools.partial(fn, done, size))
        done = done + jnp.where(hit, size, 0)
        size //= 2


def _for_granules(tot_ref, gran_ref, blk, fn):
    def one(j, carry):
        fn(pl.multiple_of(j * RUN_ALIGN, RUN_ALIGN), pl.multiple_of(gran_ref[blk * GRANULES + j], RUN_ALIGN))
        return carry
    lax.fori_loop(0, tot_ref[blk] // RUN_ALIGN, one, 0)


def _wait_rows(tot_ref, blk, wait_rows):
    _for_pieces(tot_ref[blk], LOCAL_ROWS // 2, lambda off, size: wait_rows(size))


def _dispatch_kernel(tot_ref, gran_ref, fill0_ref, filln_ref,
                     x_ref, src_ref, xs_hbm, loc, zeros, sem, zsem):
    b = pl.program_id(0)
    nb = pl.num_programs(0)
    slot = b % 2

    def copy(s, l, g, size):
        return pltpu.make_async_copy(loc.at[s, pl.ds(l, size)], xs_hbm.at[pl.ds(g, size)], sem.at[s])

    def wait_block(blk, s):
        _wait_rows(tot_ref, blk, lambda size: copy(s, 0, 0, size).wait())

    @pl.when(b >= 2)
    def _():
        wait_block(b - 2, slot)

    tok = lax.broadcasted_iota(jnp.int32, (LOCAL_ROWS, TOKEN_TILE), 1)
    pick = jnp.where(jnp.right_shift(src_ref[...], 1) == tok, 1.0, 0.0).astype(BF16)
    loc[slot] = jnp.dot(pick, x_ref[...], preferred_element_type=F32).astype(BF16)
    _for_granules(tot_ref, gran_ref, b, lambda l, g: copy(slot, l, g, RUN_ALIGN).start())

    @pl.when(b == nb - 1)
    def _():
        @pl.when(b >= 1)
        def _():
            wait_block(b - 1, 1 - slot)
        wait_block(b, slot)
        zeros[...] = jnp.zeros(zeros.shape, zeros.dtype)

        def fill(op):
            def per_expert(e, carry):
                f0 = fill0_ref[e]
                _for_pieces(filln_ref[e], MOE_ROW_TILE // 2, lambda off, size: op(pltpu.make_async_copy(
                    zeros.at[pl.ds(0, size)], xs_hbm.at[pl.ds(pl.multiple_of(f0 + off, RUN_ALIGN), size)], zsem)))
                return carry
            lax.fori_loop(0, fill0_ref.shape[0], per_expert, 0)

        fill(lambda c: c.start())
        fill(lambda c: c.wait())


def moe_dispatch(xn, plan):
    tp, d = xn.shape
    nb = tp // TOKEN_TILE
    grid_spec = pltpu.PrefetchScalarGridSpec(
        num_scalar_prefetch=4, grid=(nb,),
        in_specs=[pl.BlockSpec((TOKEN_TILE, d), lambda b, *_: (b, 0)),
                  pl.BlockSpec((None, LOCAL_ROWS, 1), lambda b, *_: (b, 0, 0))],
        out_specs=pl.BlockSpec(memory_space=pl.ANY),
        scratch_shapes=[pltpu.VMEM((2, LOCAL_ROWS, d), BF16), pltpu.VMEM((MOE_ROW_TILE // 2, d), BF16),
                        pltpu.SemaphoreType.DMA((2,)), pltpu.SemaphoreType.DMA],
    )
    return pl.pallas_call(
        _dispatch_kernel, grid_spec=grid_spec,
        out_shape=jax.ShapeDtypeStruct((plan["n_rows"], d), BF16),
        compiler_params=_cparams(("arbitrary",)), name="moe_dispatch",
    )(plan["tot"], plan["gran"], plan["fill0"], plan["filln"], xn, plan["src_col"])


def _expert_kernel(te_ref, nu_ref, first_ref, nxt_ref, par_ref,
                   xs_ref, wg_hbm, wu_hbm, wd_hbm, ys_ref,
                   wgf, wuf, wdf, wgb, wub, wdb, wsem, *, layer):
    i = pl.program_id(0)
    used = i < nu_ref[0]

    def weight_copies(e, s):
        return (pltpu.make_async_copy(wg_hbm.at[layer, e], wgf.at[s], wsem.at[s]),
                pltpu.make_async_copy(wu_hbm.at[layer, e], wuf.at[s], wsem.at[s]),
                pltpu.make_async_copy(wd_hbm.at[layer, e], wdf.at[s], wsem.at[s]))

    @pl.when(used)
    def _():
        @pl.when(first_ref[i] == 1)
        def _():
            s = par_ref[i]

            @pl.when(i == 0)
            def _():
                for c in weight_copies(te_ref[0], s):
                    c.start()

            for c in weight_copies(te_ref[i], s):
                c.wait()

            @pl.when(nxt_ref[i] >= 0)
            def _():
                for c in weight_copies(nxt_ref[i], 1 - s):
                    c.start()

            wgb[...] = wgf[s].astype(BF16)
            wub[...] = wuf[s].astype(BF16)
            wdb[...] = wdf[s].astype(BF16)

        x = xs_ref[...]
        hg = jnp.dot(x, wgb[...], preferred_element_type=F32)
        hu = jnp.dot(x, wub[...], preferred_element_type=F32)
        act = (_silu(hg) * hu).astype(BF16)
        ys_ref[...] = jnp.dot(act, wdb[...], preferred_element_type=F32).astype(ys_ref.dtype)

    @pl.when(jnp.logical_not(used))
    def _():
        ys_ref[...] = jnp.zeros(ys_ref.shape, ys_ref.dtype)


def expert_ffn(xs, plan, w_gate, w_up, w_down, layer):
    n_rows, d = xs.shape
    tm = MOE_ROW_TILE
    de = w_gate.shape[-1]

    grid_spec = pltpu.PrefetchScalarGridSpec(
        num_scalar_prefetch=5, grid=(n_rows // tm,),
        in_specs=[pl.BlockSpec((tm, d), lambda i, te, nu, *_: (jnp.minimum(i, nu[0] - 1), 0)),
                  pl.BlockSpec(memory_space=pl.ANY), pl.BlockSpec(memory_space=pl.ANY),
                  pl.BlockSpec(memory_space=pl.ANY)],
        out_specs=pl.BlockSpec((tm, d), lambda i, *_: (i, 0)),
        scratch_shapes=[pltpu.VMEM((2, d, de), F32), pltpu.VMEM((2, d, de), F32), pltpu.VMEM((2, de, d), F32),
                        pltpu.VMEM((d, de), BF16), pltpu.VMEM((d, de), BF16), pltpu.VMEM((de, d), BF16),
                        pltpu.SemaphoreType.DMA((2,))],
    )
    return pl.pallas_call(
        functools.partial(_expert_kernel, layer=layer), grid_spec=grid_spec,
        out_shape=jax.ShapeDtypeStruct((n_rows, d), BF16),
        compiler_params=_cparams(("arbitrary",)), name="expert_ffn",
    )(plan["tile_expert"], plan["n_used"], plan["tile_first"], plan["tile_next"], plan["tile_slot"],
      xs, w_gate, w_up, w_down)


def _combine_kernel(tot_ref, gran_ref,
                    ys_hbm, src_ref, wt_ref, h_ref, g_ref, *rest, n_out, emit_sum, head_blocks):
    outs, (buf, sem) = rest[:-2], rest[-2:]
    b = pl.program_id(0)
    nb = pl.num_programs(0)
    slot = b % 2

    def copy(s, l, g, size):
        return pltpu.make_async_copy(ys_hbm.at[pl.ds(g, size)], buf.at[s, pl.ds(l, size)], sem.at[s])

    def fetch(blk, s):
        _for_granules(tot_ref, gran_ref, blk, lambda l, g: copy(s, l, g, RUN_ALIGN).start())

    @pl.when(b == 0)
    def _():
        buf[...] = jnp.zeros(buf.shape, buf.dtype)
        fetch(0, 0)

    @pl.when(b + 1 < nb)
    def _():
        fetch(b + 1, 1 - slot)

    _wait_rows(tot_ref, b, lambda size: copy(slot, 0, 0, size).wait())
    y = buf[slot]
    tok2 = 2 * lax.broadcasted_iota(jnp.int32, (TOKEN_TILE, LOCAL_ROWS), 0)
    src = src_ref[...]
    h = h_ref[...]
    for k in range(2):
        pick = jnp.where(src == tok2 + k, 1.0, 0.0).astype(BF16)
        h = h + wt_ref[:, k:k + 1] * jnp.dot(pick, y, preferred_element_type=F32)
    if emit_sum:
        outs[0][...] = h
        outs = outs[1:]
    xn = h * lax.rsqrt(jnp.mean(h * h, axis=-1, keepdims=True) + RMS_EPS)
    if head_blocks is not None:
        val = xn * g_ref[0:1, :]

        @pl.when(b < head_blocks)
        def _():
            outs[0][...] = val

        @pl.when(b >= head_blocks)
        def _():
            outs[1][...] = val
        return
    for k in range(n_out):
        outs[k][...] = (xn * g_ref[k:k + 1, :]).astype(outs[k].dtype)


def moe_combine_norm(ys, plan, wts, h, gains, out_dtype, emit_sum, head_blocks=None):
    tp, d = h.shape
    nb = tp // TOKEN_TILE
    n_out = gains.shape[0]
    tok_spec = pl.BlockSpec((TOKEN_TILE, d), lambda b, *_: (b, 0))
    n_res = n_out + (1 if emit_sum else 0)
    if head_blocks is not None:
        assert n_out == 1 and not emit_sum and head_blocks == nb - 1
        out_specs = [pl.BlockSpec((TOKEN_TILE, d), lambda b, *_: (jnp.minimum(b, head_blocks - 1), 0)),
                     pl.BlockSpec((TOKEN_TILE, d), lambda b, *_: (0, 0))]
        out_shape = [jax.ShapeDtypeStruct((head_blocks * TOKEN_TILE, d), out_dtype),
                     jax.ShapeDtypeStruct((TOKEN_TILE, d), out_dtype)]
    else:
        out_specs = [tok_spec] * n_res
        out_shape = ([jax.ShapeDtypeStruct((tp, d), F32)] if emit_sum else []) + [
            jax.ShapeDtypeStruct((tp, d), out_dtype)] * n_out
    grid_spec = pltpu.PrefetchScalarGridSpec(
        num_scalar_prefetch=2, grid=(nb,),
        in_specs=[pl.BlockSpec(memory_space=pl.ANY),
                  pl.BlockSpec((None, 1, LOCAL_ROWS), lambda b, *_: (b, 0, 0)),
                  pl.BlockSpec((TOKEN_TILE, 2), lambda b, *_: (b, 0)),
                  tok_spec,
                  pl.BlockSpec((n_out, d), lambda b, *_: (0, 0))],
        out_specs=out_specs,
        scratch_shapes=[pltpu.VMEM((2, LOCAL_ROWS, d), BF16), pltpu.SemaphoreType.DMA((2,))],
    )
    return pl.pallas_call(
        functools.partial(_combine_kernel, n_out=n_out, emit_sum=emit_sum, head_blocks=head_blocks),
        grid_spec=grid_spec,
        out_shape=out_shape, compiler_params=_cparams(("arbitrary",)), name="moe_combine_norm",
    )(plan["tot"], plan["gran"], ys, plan["src_row"], wts, h, gains)


def moe_plan(idx, tp):
    i32 = jnp.int32
    tm = MOE_ROW_TILE
    nb = tp // TOKEN_TILE
    n_asg = 2 * TOKEN_TILE
    n_tiles = -(-(2 * tp + nb * N_EXPERTS * (RUN_ALIGN - 1) + N_EXPERTS * (tm - 1)) // tm)
    e_blk = idx.reshape(nb, n_asg)
    onehot = (e_blk[:, :, None] == jnp.arange(N_EXPERTS, dtype=i32)).astype(i32)
    tril = (jnp.arange(n_asg)[:, None] >= jnp.arange(n_asg)[None, :]).astype(F32)
    csum = jnp.einsum("ij,bje->bie", tril, onehot.astype(F32)).astype(i32)
    rank = jnp.sum(onehot * csum, axis=2) - 1
    plen = (csum[:, -1, :] + RUN_ALIGN - 1) // RUN_ALIGN * RUN_ALIGN
    loff = jnp.cumsum(plen, axis=1) - plen
    lpos = jnp.sum(onehot * loff[:, None, :], axis=2) + rank
    hit = lpos[:, :, None] == jnp.arange(LOCAL_ROWS, dtype=i32)
    src = jnp.sum(jnp.where(hit, jnp.arange(1, n_asg + 1, dtype=i32)[None, :, None], 0), axis=1) - 1
    seg = jnp.sum(plen, axis=0)
    seg_pad = (seg + tm - 1) // tm * tm
    gend = jnp.cumsum(seg_pad)
    gstart = gend - seg_pad
    gdst = gstart[None, :] + jnp.cumsum(plen, axis=0) - plen
    g_row = jnp.arange(GRANULES, dtype=i32) * RUN_ALIGN
    run_of = jnp.sum(((loff + plen)[:, None, :] <= g_row[None, :, None]).astype(i32), axis=2)
    run_hot = (run_of[:, :, None] == jnp.arange(N_EXPERTS, dtype=i32)).astype(i32)
    gran = jnp.sum(run_hot * (gdst - loff)[:, None, :], axis=2) + g_row[None, :]
    tot = jnp.sum(plen, axis=1)
    n_used = gend[-1] // tm
    t_row = jnp.minimum(jnp.arange(n_tiles, dtype=i32), n_used - 1) * tm
    tile_expert = jnp.minimum(jnp.sum(gend[None, :] <= t_row[:, None], axis=1), N_EXPERTS - 1)
    ids = jnp.arange(N_EXPERTS, dtype=i32)
    has = seg > 0
    later = jnp.logical_and(ids[None, :] > ids[:, None], has[None, :])
    nxt_e = jnp.min(jnp.where(later, ids[None, :], N_EXPERTS), axis=1)
    nxt_e = jnp.where(nxt_e < N_EXPERTS, nxt_e, -1)
    slot_e = (jnp.cumsum(has.astype(i32)) - 1) % 2
    te_hot = (tile_expert[:, None] == ids[None, :]).astype(i32)
    tile_first = jnp.concatenate([jnp.ones((1,), i32), (tile_expert[1:] != tile_expert[:-1]).astype(i32)])
    plan_tiles = {"tile_first": tile_first, "tile_next": jnp.sum(te_hot * nxt_e[None, :], axis=1),
                  "tile_slot": jnp.sum(te_hot * slot_e[None, :], axis=1)}
    return {
        **{k: v.astype(i32) for k, v in plan_tiles.items()},
        "tot": tot.astype(i32), "gran": gran.reshape(-1).astype(i32),
        "fill0": jnp.concatenate([gstart + seg, gend[-1:]]).astype(i32),
        "filln": jnp.concatenate([seg_pad - seg, n_tiles * tm - gend[-1:]]).astype(i32),
        "src_col": src[:, :, None].astype(i32), "src_row": src[:, None, :].astype(i32),
        "tile_expert": tile_expert.astype(i32), "n_used": n_used.reshape(1).astype(i32),
        "n_rows": n_tiles * tm,
    }


def kernel(x_prompt, x_sample, state_conv, state_ssm, cache_k, cache_v, a_norm, a_w_in, a_conv_w, a_conv_b, a_dt_bias, a_a_log, a_d_skip, a_gate_norm, a_w_out, kv_norm, w_kv, b_norm, b_w_q, b_w_o, moe_norm, router_group_w, router_group_b, router_expert_w, router_expert_b, expert_w_gate, expert_w_up, expert_w_down, final_norm):
    n_pb, seq, d_model = x_prompt.shape
    n_sb, n_new, _ = x_sample.shape
    n_a = a_w_in.shape[0]
    depth = moe_norm.shape[0]
    d_inner = a_w_out.shape[1]
    conv_dim = a_conv_w.shape[2]
    gn = (conv_dim - d_inner) // 2
    n_heads = d_inner // SSM_HEAD_DIM
    d_att = w_kv.shape[1] // 2
    n_p, n_s = n_pb * seq, n_sb * n_new
    lcm = MM_ROW_TILE * TOKEN_TILE // math.gcd(MM_ROW_TILE, TOKEN_TILE)
    tp = -(-(n_p + n_s) // lcm) * lcm
    L = SSD_CHUNK
    assert n_heads <= V7X_LANES and n_new <= L and n_new >= CONV_W - 1 and seq % (max(DILATIONS) * ATTN_Q_BLOCK) == 0

    def pad_rows(parts):
        rows = sum(p.shape[0] for p in parts)
        return jnp.concatenate(parts + [jnp.zeros((tp - rows,) + parts[0].shape[1:], parts[0].dtype)], axis=0)

    def lanes(v):
        return jnp.pad(v.astype(F32), ((0, 0), (0, V7X_LANES - n_heads)))[:, None, :]

    head_of_col = jnp.arange(d_inner, dtype=jnp.int32) // SSM_HEAD_DIM
    ssd_prm = {
        "conv_w": a_conv_w, "conv_b": a_conv_b[:, None, :],
        "dt_bias": lanes(a_dt_bias), "a_log": lanes(a_a_log),
        "d_skip": jnp.repeat(a_d_skip.astype(F32), SSM_HEAD_DIM, axis=1)[:, None, :],
        "gate_norm": a_gate_norm[:, None, :],
        "expand": (jnp.arange(V7X_LANES, dtype=jnp.int32)[:, None] == head_of_col[None, :]).astype(BF16),
    }
    w_dt = jnp.pad(a_w_in[:, :, 2 * d_inner + 2 * gn:], ((0, 0), (0, 0), (0, V7X_LANES - n_heads)))
    w_in_nk = jnp.swapaxes(a_w_in, 1, 2)
    r_w = jnp.concatenate([router_group_w, router_expert_w.reshape(depth, d_model, N_EXPERTS)], axis=2)
    r_w = jnp.pad(r_w, ((0, 0), (0, 0), (0, V7X_LANES - r_w.shape[2])))
    r_b = jnp.concatenate([router_group_b, router_expert_b.reshape(depth, N_EXPERTS)], axis=1)
    r_b = jnp.pad(r_b, ((0, 0), (0, V7X_LANES - r_b.shape[1])))[:, None, :]

    h = pad_rows([x_prompt.reshape(n_p, d_model), x_sample.reshape(n_s, d_model)])
    (xn,) = add_norm(h, [], a_norm[0:1], BF16, emit_sum=False)
    conv_p, ssm_p, conv_s, ssm_s = [], [], [], []
    kv = None
    for layer in range(depth):
        if layer < n_a:
            zxbc = matmul(xn, w_in_nk, layer, 2 * d_inner + 2 * gn, 1024, w_is_nk=True)
            dt_raw = matmul(xn, w_dt, layer, V7X_LANES, V7X_LANES)
            y_p, cp, sp = ssd_mixer(zxbc, dt_raw, 0, n_pb, seq // L, L, None, None, ssd_prm, layer, d_inner, gn)
            precise = layer == 0
            if precise:
                x_s = x_sample.reshape(n_s, d_model)
                (xn_s,) = add_norm(x_s, [], a_norm[0:1], F32, emit_sum=False)
                zx_s = matmul_f32(xn_s, w_in_nk, layer, 2 * d_inner + 2 * gn, 1024, w_is_nk=True)
                dt_s = matmul_f32(xn_s, w_dt, layer, V7X_LANES, V7X_LANES)
            else:
                zx_s, dt_s = zxbc[n_p:n_p + n_s], dt_raw[n_p:n_p + n_s]
            zs = jnp.pad(zx_s.reshape(n_sb, n_new, -1), ((0, 0), (0, L - n_new), (0, 0)))
            ds = jnp.pad(dt_s.reshape(n_sb, n_new, -1), ((0, 0), (0, L - n_new), (0, 0)))
            y_s, cs, ss = ssd_mixer(zs.reshape(n_sb * L, -1), ds.reshape(n_sb * L, -1), 0, n_sb, 1, n_new,
                                    state_conv[layer], state_ssm[layer].reshape(n_sb, n_heads * SSM_HEAD_DIM, D_STATE),
                                    ssd_prm, layer, d_inner, gn, precise=precise)
            conv_p.append(cp); ssm_p.append(sp); conv_s.append(cs); ssm_s.append(ss)
            y_s = y_s.reshape(n_sb, L, d_inner)[:, :n_new].reshape(n_s, d_inner)
            y = pad_rows([y_p, y_s.astype(BF16)])
            h = matmul(y, a_w_out, layer, d_model, 512, res=h)
            if precise:
                h_s = matmul_f32(y_s, a_w_out, layer, d_model, 512, res=x_s)
                h = lax.dynamic_update_slice(h, h_s, (n_p, 0))
        else:
            b = layer - n_a
            q = matmul(xn, b_w_q, b, b_w_q.shape[2], 1024)
            o_p = attn_prompt(q, kv, n_pb, seq)
            qs = q[n_p:n_p + n_s].reshape(n_sb, n_new, len(DILATIONS), N_KV_HEADS, ATT_HEAD_DIM)
            qs = qs.transpose(0, 2, 3, 1, 4).reshape(n_sb, len(DILATIONS) * N_KV_HEADS * n_new, ATT_HEAD_DIM)
            qs = jnp.pad(qs, ((0, 0), (0, V7X_LANES - qs.shape[1]), (0, 0)))
            kvs = kv[n_p:n_p + n_s].reshape(n_sb, n_new, 2, N_KV_HEADS * ATT_HEAD_DIM).transpose(2, 0, 1, 3)
            kvs = kvs.reshape(2, n_sb, n_new * N_KV_HEADS, ATT_HEAD_DIM)
            kvs = jnp.pad(kvs, ((0, 0), (0, 0), (0, V7X_LANES - n_new * N_KV_HEADS), (0, 0)))
            o_s = attn_sample(qs, kvs[0], kvs[1], cache_k.reshape(n_sb, -1, ATT_HEAD_DIM),
                              cache_v.reshape(n_sb, -1, ATT_HEAD_DIM), n_new)
            o_s = o_s.reshape(n_sb, N_KV_HEADS, n_new, ATT_HEAD_DIM).transpose(0, 2, 1, 3).reshape(n_s, d_att)
            o = pad_rows([o_p, o_s])
            h = matmul(o, b_w_o, b, d_model, 1024, res=h)
        xm, idx, wts = router(h, moe_norm[layer:layer + 1], r_w[layer], r_b[layer])
        plan = moe_plan(idx, tp)
        ys = expert_ffn(moe_dispatch(xm, plan), plan, expert_w_gate, expert_w_up, expert_w_down, layer)
        if layer + 1 < n_a:
            h, xn = moe_combine_norm(ys, plan, wts, h, a_norm[layer + 1:layer + 2], BF16, emit_sum=True)
        elif layer + 1 == n_a:
            h, xkv, xn = moe_combine_norm(ys, plan, wts, h, jnp.stack([kv_norm, b_norm[0]]), BF16, emit_sum=True)
            kv = matmul(xkv, w_kv, None, w_kv.shape[1], 1024)
        elif layer + 1 < depth:
            h, xn = moe_combine_norm(ys, plan, wts, h, b_norm[layer + 1 - n_a:layer + 2 - n_a], BF16,
                                     emit_sum=True)
        else:
            y_head, y_tail = moe_combine_norm(ys, plan, wts, h, final_norm[None, :], F32, emit_sum=False,
                                              head_blocks=n_p // TOKEN_TILE)

    y_prompt = y_head.reshape(n_pb, seq, d_model)
    y_sample = y_tail[:n_s].reshape(n_sb, n_new, d_model)
    keep = min(max(DILATIONS) * (N_DIL_KEYS - 1), seq)
    kv_p = kv[:n_p].reshape(n_pb, seq, 2, N_KV_HEADS, ATT_HEAD_DIM)[:, seq - keep:]
    kv_s = kv[n_p:n_p + n_s].reshape(n_sb, n_new, 2, N_KV_HEADS, ATT_HEAD_DIM)
    ssm_shape = (n_heads, SSM_HEAD_DIM, D_STATE)
    return (y_prompt, y_sample,
            jnp.stack(conv_p), jnp.stack(ssm_p).reshape((n_a, n_pb) + ssm_shape),
            kv_p[:, :, 0], kv_p[:, :, 1],
            jnp.stack(conv_s), jnp.stack(ssm_s).reshape((n_a, n_sb) + ssm_shape),
            kv_s[:, :, 0], kv_s[:, :, 1])
```

```python
import functools
import math

import jax
import jax.numpy as jnp
from jax import lax
from jax.experimental import pallas as pl
from jax.experimental.pallas import tpu as pltpu

F32 = jnp.float32
BF16 = jnp.bfloat16

V7X_VMEM_BYTES = 64 * 1024 * 1024
V7X_LANES = 128
V7X_SUBLANES = 8
VMEM_LIMIT = 56 * 1024 * 1024

RMS_EPS = 1e-6
SSM_HEAD_DIM = 64
N_SSM_GROUPS = 8
D_STATE = 128
CONV_W = 4
SSD_CHUNK = 128
ATT_HEAD_DIM = 128
N_KV_HEADS = 8
DILATIONS = (1, 4, 16)
N_DIL_KEYS = 129
assert all(d & (d - 1) == 0 for d in DILATIONS)
N_EXPERT_GROUPS = 4
EXPERTS_PER_GROUP = 8
N_EXPERTS = N_EXPERT_GROUPS * EXPERTS_PER_GROUP
NEG_INF = -1e30

TOKEN_TILE = 256
MM_ROW_TILE = 768
MM_ROW_TILE_WIDE = 1408
MOE_ROW_TILE = 256
ATTN_Q_BLOCK = 256
ATTN_UNROLL = 4


def _cparams(sem):
    return pltpu.CompilerParams(dimension_semantics=sem, vmem_limit_bytes=VMEM_LIMIT)


def _norm_kernel(*refs, n_add, n_out, emit_sum):
    h = refs[0][...]
    for a in refs[1:1 + n_add]:
        h = h + a[...]
    g_ref = refs[1 + n_add]
    outs = refs[2 + n_add:]
    if emit_sum:
        outs[0][...] = h
        outs = outs[1:]
    xn = h * lax.rsqrt(jnp.mean(h * h, axis=-1, keepdims=True) + RMS_EPS)
    for k in range(n_out):
        outs[k][...] = (xn * g_ref[k:k + 1, :]).astype(outs[k].dtype)


def add_norm(h, adds, gains, out_dtype, emit_sum):
    tp, d = h.shape
    n_out = gains.shape[0]
    tm = min(TOKEN_TILE, tp)
    in_specs = [pl.BlockSpec((tm, d), lambda i: (i, 0))]
    args = [h]
    for arr, off in adds:
        in_specs.append(pl.BlockSpec((tm, d), functools.partial(lambda i, o: (i + o, 0), o=off)))
        args.append(arr)
    in_specs.append(pl.BlockSpec((n_out, d), lambda i: (0, 0)))
    args.append(gains)
    out_shape, out_specs = [], []
    if emit_sum:
        out_shape.append(jax.ShapeDtypeStruct((tp, d), F32))
        out_specs.append(pl.BlockSpec((tm, d), lambda i: (i, 0)))
    for _ in range(n_out):
        out_shape.append(jax.ShapeDtypeStruct((tp, d), out_dtype))
        out_specs.append(pl.BlockSpec((tm, d), lambda i: (i, 0)))
    return pl.pallas_call(
        functools.partial(_norm_kernel, n_add=len(adds), n_out=n_out, emit_sum=emit_sum),
        grid=(tp // tm,), in_specs=in_specs, out_specs=out_specs, out_shape=out_shape,
        compiler_params=_cparams(("parallel",)), name="add_norm",
    )(*args)


def _mm_kernel(*refs, has_res, w_is_nk):
    if has_res:
        x_ref, w_ref, r_ref, o_ref, wb_ref = refs
    else:
        x_ref, w_ref, o_ref, wb_ref = refs

    @pl.when(pl.program_id(1) == 0)
    def _():
        wb_ref[...] = w_ref[...].astype(BF16)

    contract = (((1,), (1 if w_is_nk else 0,)), ((), ()))
    acc = lax.dot_general(x_ref[...].astype(BF16), wb_ref[...], contract, preferred_element_type=F32)
    if has_res:
        acc = r_ref[...] + acc
    o_ref[...] = acc


def matmul(x, w, layer, n_cols, tn, res=None, tm=MM_ROW_TILE, w_is_nk=False):
    tp, k = x.shape
    assert tp % tm == 0 and n_cols % tn == 0
    if w_is_nk:
        w_spec = pl.BlockSpec((None, tn, k), lambda j, i: (layer, j, 0))
    elif layer is None:
        w_spec = pl.BlockSpec((k, tn), lambda j, i: (0, j))
    else:
        w_spec = pl.BlockSpec((None, k, tn), lambda j, i: (layer, 0, j))
    in_specs = [pl.BlockSpec((tm, k), lambda j, i: (i, 0)), w_spec]
    args = [x, w]
    if res is not None:
        in_specs.append(pl.BlockSpec((tm, tn), lambda j, i: (i, j)))
        args.append(res)
    return pl.pallas_call(
        functools.partial(_mm_kernel, has_res=res is not None, w_is_nk=w_is_nk),
        grid=(n_cols // tn, tp // tm), in_specs=in_specs,
        out_specs=pl.BlockSpec((tm, tn), lambda j, i: (i, j)),
        out_shape=jax.ShapeDtypeStruct((tp, n_cols), F32),
        scratch_shapes=[pltpu.VMEM((tn, k) if w_is_nk else (k, tn), BF16)],
        compiler_params=_cparams(("arbitrary", "arbitrary")), name="proj",
    )(*args)


def _mm_f32_kernel(*refs, has_res, w_is_nk):
    if has_res:
        x_ref, w_ref, r_ref, o_ref = refs
    else:
        x_ref, w_ref, o_ref = refs
    contract = (((1,), (1 if w_is_nk else 0,)), ((), ()))
    acc = lax.dot_general(x_ref[...], w_ref[...], contract, preferred_element_type=F32,
                          precision=lax.Precision.HIGHEST)
    if has_res:
        acc = r_ref[...] + acc
    o_ref[...] = acc


def matmul_f32(x, w, layer, n_cols, tn, res=None, w_is_nk=False):
    m, k = x.shape
    if w_is_nk:
        w_spec = pl.BlockSpec((None, tn, k), lambda j: (layer, j, 0))
    else:
        w_spec = pl.BlockSpec((None, k, tn), lambda j: (layer, 0, j))
    in_specs = [pl.BlockSpec((m, k), lambda j: (0, 0)), w_spec]
    args = [x, w]
    if res is not None:
        in_specs.append(pl.BlockSpec((m, tn), lambda j: (0, j)))
        args.append(res)
    return pl.pallas_call(
        functools.partial(_mm_f32_kernel, has_res=res is not None, w_is_nk=w_is_nk),
        grid=(n_cols // tn,), in_specs=in_specs, out_specs=pl.BlockSpec((m, tn), lambda j: (0, j)),
        out_shape=jax.ShapeDtypeStruct((m, n_cols), F32),
        compiler_params=_cparams(("parallel",)), name="proj_f32",
    )(*args)


def _silu(x):
    return x * (0.5 * jnp.tanh(0.5 * x) + 0.5)


def _split2(v):
    hi = v.astype(BF16)
    lo = (v - hi.astype(F32)).astype(BF16)
    return hi, lo


def _ssd_kernel(z_ref, x_ref, b_ref, c_ref, dt_ref, conv0_ref, ssm0_ref,
                cw_ref, cb_ref, dtb_ref, alog_ref, dsk_ref, gn_ref, e_ref,
                y_ref, convo_ref, ssmo_ref,
                xpad, xs_s, xlo_s, xhi_s, xw_s, bm_s, bmt_s, cm_s, y_s, ea_s, state_s, dec_s,
                *, n_valid, has_init):
    L = SSD_CHUNK
    mx = xlo_s.dtype
    prec = lax.Precision.HIGHEST if mx == F32 else None
    d_inner = xs_s.shape[1]
    gn = bm_s.shape[1]
    n_heads = d_inner // SSM_HEAD_DIM
    pair = 2 * SSM_HEAD_DIM
    gcols = d_inner // N_SSM_GROUPS
    c = pl.program_id(1)
    pad0 = V7X_SUBLANES
    keep = CONV_W - 1

    @pl.when(c == 0)
    def _():
        if has_init:
            xpad[pad0 - keep:pad0, :] = conv0_ref[...]
            for g in range(N_SSM_GROUPS):
                state_s[:, g * gcols:(g + 1) * gcols] = ssm0_ref[g * gcols:(g + 1) * gcols, :].T
        else:
            xpad[0:pad0, :] = jnp.zeros((pad0, xpad.shape[1]), F32)
            state_s[...] = jnp.zeros(state_s.shape, F32)

    xpad[pad0:pad0 + L, 0:d_inner] = x_ref[...]
    xpad[pad0:pad0 + L, d_inner:d_inner + gn] = b_ref[...]
    xpad[pad0:pad0 + L, d_inner + gn:d_inner + 2 * gn] = c_ref[...]
    strip = D_STATE
    lane = lax.broadcasted_iota(jnp.int32, (L, strip), 1)
    even_head = (lane // SSM_HEAD_DIM) % 2 == 0
    for s in range(xpad.shape[1] // strip):
        cs = slice(s * strip, (s + 1) * strip)
        rows = xpad[:, cs]
        acc = cb_ref[:, cs] + cw_ref[keep:keep + 1, cs] * rows[pad0:pad0 + L]
        for k in range(keep):
            acc = acc + cw_ref[k:k + 1, cs] * pltpu.roll(rows, keep - k, 0)[pad0:pad0 + L]
        v = _silu(acc)
        if s * strip < d_inner:
            xs_s[:, cs] = v
            vb = v.astype(mx)
            xlo_s[:, cs] = jnp.where(even_head, vb, jnp.zeros_like(vb))
            xhi_s[:, cs] = jnp.where(even_head, jnp.zeros_like(vb), vb)
        elif s * strip < d_inner + gn:
            o = s * strip - d_inner
            bm_s[:, o:o + strip] = v.astype(mx)
            for q in range(strip // D_STATE):
                gi = o // D_STATE + q
                bmt_s[:, gi * L:(gi + 1) * L] = v[:, q * D_STATE:(q + 1) * D_STATE].T.astype(mx)
        else:
            o = s * strip - d_inner - gn
            cm_s[:, o:o + strip] = v.astype(mx)

    tail = xpad[pad0 + n_valid - keep:pad0 + n_valid, :]
    xpad[pad0 - keep:pad0, :] = tail

    @pl.when(c == pl.num_programs(1) - 1)
    def _():
        convo_ref[...] = tail

    row = lax.broadcasted_iota(jnp.int32, (L, V7X_LANES), 0)
    dtv = dt_ref[...] + dtb_ref[...]
    dt = jnp.maximum(dtv, 0.0) + jnp.log(1.0 + jnp.exp(-jnp.abs(dtv)))
    dt = jnp.where(row < n_valid, dt, 0.0)
    a = -jnp.exp(alog_ref[...])
    adt = dt * a
    ii = lax.broadcasted_iota(jnp.int32, (L, L), 0)
    jj = lax.broadcasted_iota(jnp.int32, (L, L), 1)
    causal = ii >= jj
    tri = jnp.where(causal, 1.0, 0.0).astype(F32)
    a_cs = jnp.dot(tri, adt, preferred_element_type=F32, precision=lax.Precision.HIGHEST)
    dec_s[0] = a_cs
    dec_s[1] = a_cs.T
    dec_s[2] = dt.T
    dec_s[3] = jnp.where(causal, 0.0, NEG_INF).astype(F32)
    ea = jnp.exp(a_cs)
    dte = dt * jnp.exp(a_cs[L - 1:L, :] - a_cs)
    ea_p, dte_p = _split2(ea), _split2(dte)
    ex_strip = 256
    for s in range(d_inner // ex_strip):
        cs = slice(s * ex_strip, (s + 1) * ex_strip)

        def expand(v, pieces):
            if mx == F32:
                return jnp.dot(v, e_ref[:, cs].astype(F32), preferred_element_type=F32, precision=prec)
            return (jnp.dot(pieces[0], e_ref[:, cs], preferred_element_type=F32)
                    + jnp.dot(pieces[1], e_ref[:, cs], preferred_element_type=F32))

        ea_s[:, cs] = expand(ea, ea_p)
        xw_s[:, cs] = (xs_s[:, cs] * expand(dte, dte_p)).astype(mx)

    for g in range(N_SSM_GROUPS):
        ns = slice(g * D_STATE, (g + 1) * D_STATE)
        cg = cm_s[:, ns]
        cb = lax.dot_general(cg, bm_s[:, ns], (((1,), (1,)), ((), ())), preferred_element_type=F32,
                             precision=prec)
        for q in range(gcols // pair):
            col = slice(g * gcols + q * pair, g * gcols + (q + 1) * pair)
            acc = None
            for hh, xsrc in ((0, xlo_s), (1, xhi_s)):
                h = (g * gcols + q * pair) // SSM_HEAD_DIM + hh
                seg = dec_s[0, :, h:h + 1] - dec_s[1, h:h + 1, :]
                m = (cb * jnp.exp(seg + dec_s[3]) * dec_s[2, h:h + 1, :]).astype(mx)
                part = jnp.dot(m, xsrc[:, col], preferred_element_type=F32, precision=prec)
                acc = part if acc is None else acc + part
            st = state_s[:, col]
            y_off = jnp.dot(cg, st.astype(mx), preferred_element_type=F32, precision=prec)
            y_s[:, col] = acc + y_off * ea_s[:, col] + dsk_ref[:, col] * xs_s[:, col]
            state_s[:, col] = st * ea_s[L - 1:L, col] + jnp.dot(
                bmt_s[:, g * L:(g + 1) * L], xw_s[:, col], preferred_element_type=F32, precision=prec)

    for g in range(N_SSM_GROUPS):
        gs = slice(g * gcols, (g + 1) * gcols)
        v = y_s[:, gs] * _silu(z_ref[:, gs])
        v = v * lax.rsqrt(jnp.mean(v * v, axis=-1, keepdims=True) + RMS_EPS)
        y_ref[:, gs] = (v * gn_ref[:, gs]).astype(y_ref.dtype)

    @pl.when(c == pl.num_programs(1) - 1)
    def _():
        for g in range(N_SSM_GROUPS):
            ssmo_ref[g * gcols:(g + 1) * gcols, :] = state_s[:, g * gcols:(g + 1) * gcols].T


def ssd_mixer(zxbc, dt_raw, row0, n_seq, n_chunks, n_valid, conv0, ssm0, prm, layer, d_inner, gn,
              precise=False):
    L = SSD_CHUNK
    assert row0 % L == 0
    rb = row0 // L
    mx = F32 if precise else BF16
    conv_dim = d_inner + 2 * gn
    has_init = conv0 is not None
    if not has_init:
        conv0 = jnp.zeros((n_seq, CONV_W - 1, conv_dim), F32)
        ssm0 = jnp.zeros((n_seq, 8, D_STATE), F32)
    n_heads = d_inner // SSM_HEAD_DIM
    rows = n_seq * n_chunks * L

    def rmap(b, c):
        return rb + b * n_chunks + c

    zb = d_inner // gn
    in_specs = [
        pl.BlockSpec((L, d_inner), lambda b, c: (rmap(b, c), 0)),
        pl.BlockSpec((L, d_inner), lambda b, c: (rmap(b, c), 1)),
        pl.BlockSpec((L, gn), lambda b, c: (rmap(b, c), 2 * zb)),
        pl.BlockSpec((L, gn), lambda b, c: (rmap(b, c), 2 * zb + 1)),
        pl.BlockSpec((L, V7X_LANES), lambda b, c: (rmap(b, c), 0)),
        pl.BlockSpec((None, CONV_W - 1, conv_dim), lambda b, c: (b, 0, 0)),
        pl.BlockSpec((None,) + ssm0.shape[1:], lambda b, c: (b, 0, 0)),
        pl.BlockSpec((None, CONV_W, conv_dim), lambda b, c: (layer, 0, 0)),
        pl.BlockSpec((None, 1, conv_dim), lambda b, c: (layer, 0, 0)),
        pl.BlockSpec((None, 1, V7X_LANES), lambda b, c: (layer, 0, 0)),
        pl.BlockSpec((None, 1, V7X_LANES), lambda b, c: (layer, 0, 0)),
        pl.BlockSpec((None, 1, d_inner), lambda b, c: (layer, 0, 0)),
        pl.BlockSpec((None, 1, d_inner), lambda b, c: (layer, 0, 0)),
        pl.BlockSpec((V7X_LANES, d_inner), lambda b, c: (0, 0)),
    ]
    out_shape = [
        jax.ShapeDtypeStruct((rows, d_inner), mx),
        jax.ShapeDtypeStruct((n_seq, CONV_W - 1, conv_dim), F32),
        jax.ShapeDtypeStruct((n_seq, n_heads * SSM_HEAD_DIM, D_STATE), F32),
    ]
    out_specs = [
        pl.BlockSpec((L, d_inner), lambda b, c: (b * n_chunks + c, 0)),
        pl.BlockSpec((None, CONV_W - 1, conv_dim), lambda b, c: (b, 0, 0)),
        pl.BlockSpec((None, n_heads * SSM_HEAD_DIM, D_STATE), lambda b, c: (b, 0, 0)),
    ]
    scratch = [
        pltpu.VMEM((L + V7X_SUBLANES, conv_dim), F32),
        pltpu.VMEM((L, d_inner), F32),
        pltpu.VMEM((L, d_inner), mx),
        pltpu.VMEM((L, d_inner), mx),
        pltpu.VMEM((L, d_inner), mx),
        pltpu.VMEM((L, gn), mx),
        pltpu.VMEM((D_STATE, N_SSM_GROUPS * L), mx),
        pltpu.VMEM((L, gn), mx),
        pltpu.VMEM((L, d_inner), F32),
        pltpu.VMEM((L, d_inner), F32),
        pltpu.VMEM((D_STATE, d_inner), F32),
        pltpu.VMEM((4, L, V7X_LANES), F32),
    ]
    return pl.pallas_call(
        functools.partial(_ssd_kernel, n_valid=n_valid, has_init=has_init),
        grid=(n_seq, n_chunks), in_specs=in_specs, out_specs=out_specs, out_shape=out_shape,
        scratch_shapes=scratch,
        compiler_params=_cparams(("arbitrary", "arbitrary")), name="ssd_mixer",
    )(zxbc, zxbc, zxbc, zxbc, dt_raw, conv0, ssm0,
      prm["conv_w"], prm["conv_b"], prm["dt_bias"], prm["a_log"], prm["d_skip"], prm["gate_norm"],
      prm["expand"])


def _attn_prompt_kernel(q0_ref, q1_ref, q2_ref, k_ref, v_ref, o_ref, *scratch, seq):
    og_s, lse_s = scratch[:len(DILATIONS)], scratch[len(DILATIONS):]
    QB = ATTN_Q_BLOCK
    W = N_DIL_KEYS - 1
    scale = ATT_HEAD_DIM ** -0.5
    dist_c = (lax.broadcasted_iota(jnp.int32, (QB, QB), 0) - lax.broadcasted_iota(jnp.int32, (QB, QB), 1))
    cur_ok = jnp.logical_and(dist_c >= 0, dist_c <= W)
    kq = lax.broadcasted_iota(jnp.int32, (QB, W), 1) - lax.broadcasted_iota(jnp.int32, (QB, W), 0)

    def scores(q, k_rows, ok):
        s = lax.dot_general(q, k_rows.astype(BF16), (((1,), (1,)), ((), ())),
                            preferred_element_type=F32) * scale
        return jnp.where(ok, s, NEG_INF)

    for g, (q_ref, d) in enumerate(zip((q0_ref, q1_ref, q2_ref), DILATIONS)):
        nb_r = seq // d // QB

        def body(idx, carry, g=g, q_ref=q_ref, d=d, nb_r=nb_r):
            r = idx // nb_r
            mb = idx % nb_r
            start = r + d * QB * mb
            rows = pl.ds(start, QB, stride=d) if d > 1 else pl.ds(start, QB)
            q = q_ref[rows, :].astype(BF16)
            s_c = scores(q, k_ref[rows, :], cur_ok)
            m = jnp.max(s_c, axis=-1, keepdims=True)
            if nb_r > 1:
                pstart = jnp.maximum(start - d * W, r)
                prev_min = jnp.where(mb > 0, 0, W)
                prows = pl.ds(pstart, W, stride=d) if d > 1 else pl.ds(pstart, W)
                s_p = scores(q, k_ref[prows, :], kq >= prev_min)
                m = jnp.maximum(m, jnp.max(s_p, axis=-1, keepdims=True))
            p_c = jnp.exp(s_c - m)
            den = jnp.sum(p_c, axis=-1, keepdims=True)
            pv = jnp.dot(p_c.astype(BF16), v_ref[rows, :].astype(BF16), preferred_element_type=F32)
            if nb_r > 1:
                p_p = jnp.exp(s_p - m)
                den = den + jnp.sum(p_p, axis=-1, keepdims=True)
                pv = pv + jnp.dot(p_p.astype(BF16), v_ref[prows, :].astype(BF16), preferred_element_type=F32)
            og_s[g][rows, :] = pv / den
            lse_s[g][rows, :] = jnp.broadcast_to(m + jnp.log(den), (QB, ATT_HEAD_DIM))
            return carry

        n_blk = seq // QB
        lax.fori_loop(0, n_blk, body, 0, unroll=min(ATTN_UNROLL, n_blk))

    QB = 128

    def merge(i, carry):
        rows = pl.ds(pl.multiple_of(i * QB, QB), QB)
        lse = [lse_s[g][rows, :] for g in range(len(DILATIONS))]
        top = functools.reduce(jnp.maximum, lse)
        w = [jnp.exp(x - top) for x in lse]
        num = functools.reduce(lambda a, b: a + b, [w[g] * og_s[g][rows, :] for g in range(len(DILATIONS))])
        o_ref[rows, :] = num / functools.reduce(lambda a, b: a + b, w)
        return carry

    lax.fori_loop(0, seq // QB, merge, 0, unroll=ATTN_UNROLL)


def attn_prompt(q, kv, n_seq, seq):
    hd = ATT_HEAD_DIM
    nh = N_KV_HEADS

    def qspec(g):
        return pl.BlockSpec((seq, hd), lambda b, h: (b, g * nh + h))

    return pl.pallas_call(
        functools.partial(_attn_prompt_kernel, seq=seq),
        grid=(n_seq, nh),
        in_specs=[qspec(0), qspec(1), qspec(2),
                  pl.BlockSpec((seq, hd), lambda b, h: (b, h)),
                  pl.BlockSpec((seq, hd), lambda b, h: (b, nh + h))],
        out_specs=pl.BlockSpec((seq, hd), lambda b, h: (b, h)),
        out_shape=jax.ShapeDtypeStruct((n_seq * seq, nh * hd), F32),
        scratch_shapes=[pltpu.VMEM((seq, hd), F32)] * (2 * len(DILATIONS)),
        compiler_params=_cparams(("parallel", "parallel")), name="attn_prompt",
    )(q, q, q, kv, kv)


def _attn_sample_kernel(q_ref, kn_ref, vn_ref, kc_ref, vc_ref, o_ref, *, n_new, past):
    QR = q_ref.shape[0]
    scale = ATT_HEAD_DIM ** -0.5
    q = q_ref[...].astype(BF16)

    def dist_ok(dist, dil):
        return jnp.logical_and(jnp.logical_and(dist >= 0, dist <= dil * (N_DIL_KEYS - 1)),
                               jnp.bitwise_and(dist, dil - 1) == 0)

    def mask(n_keys, key0):
        rr = lax.broadcasted_iota(jnp.int32, (QR, n_keys), 0)
        kpos = lax.broadcasted_iota(jnp.int32, (QR, n_keys), 1) + key0
        g = rr // n_new
        qpos = past + rr % n_new
        dil = jnp.where(g == 0, DILATIONS[0], jnp.where(g == 1, DILATIONS[1], DILATIONS[2]))
        ok = dist_ok(qpos - kpos, dil)
        return jnp.logical_and(ok, g < len(DILATIONS))

    ok_c = mask(past, 0)
    ok_n = jnp.logical_and(mask(kn_ref.shape[0], past),
                           lax.broadcasted_iota(jnp.int32, (QR, kn_ref.shape[0]), 1) < n_new)
    head_rows = pl.ds(pl.program_id(1), past, stride=N_KV_HEADS)
    s_c = lax.dot_general(q, kc_ref[head_rows, :].astype(BF16), (((1,), (1,)), ((), ())),
                          preferred_element_type=F32) * scale
    s_n = lax.dot_general(q, kn_ref[...].astype(BF16), (((1,), (1,)), ((), ())),
                          preferred_element_type=F32) * scale
    s_c = jnp.where(ok_c, s_c, NEG_INF)
    s_n = jnp.where(ok_n, s_n, NEG_INF)
    m = jnp.maximum(jnp.max(s_c, axis=-1, keepdims=True), jnp.max(s_n, axis=-1, keepdims=True))
    p_c = jnp.exp(s_c - m)
    p_n = jnp.exp(s_n - m)
    l = jnp.sum(p_c, axis=-1, keepdims=True) + jnp.sum(p_n, axis=-1, keepdims=True)
    pv = (jnp.dot(p_c.astype(BF16), vc_ref[head_rows, :].astype(BF16), preferred_element_type=F32)
          + jnp.dot(p_n.astype(BF16), vn_ref[...].astype(BF16), preferred_element_type=F32))
    ms = [m[g * n_new:(g + 1) * n_new] for g in range(len(DILATIONS))]
    m_all = functools.reduce(jnp.maximum, ms)
    num = jnp.zeros((n_new, ATT_HEAD_DIM), F32)
    den = jnp.zeros((n_new, 1), F32)
    for g in range(len(DILATIONS)):
        w = jnp.exp(ms[g] - m_all)
        num = num + w * pv[g * n_new:(g + 1) * n_new]
        den = den + w * l[g * n_new:(g + 1) * n_new]
    o_ref[...] = num / den


def attn_sample(q_s, k_new, v_new, cache_k, cache_v, n_new):
    nb, nh, qr, hd = q_s.shape
    past = cache_k.shape[1] // nh
    slab = pl.BlockSpec((None, past * nh, hd), lambda b, h: (b, 0, 0))
    return pl.pallas_call(
        functools.partial(_attn_sample_kernel, n_new=n_new, past=past),
        grid=(nb, nh),
        in_specs=[pl.BlockSpec((None, None, qr, hd), lambda b, h: (b, h, 0, 0)),
                  pl.BlockSpec((None, None, k_new.shape[2], hd), lambda b, h: (b, h, 0, 0)),
                  pl.BlockSpec((None, None, k_new.shape[2], hd), lambda b, h: (b, h, 0, 0)),
                  slab, slab],
        out_specs=pl.BlockSpec((None, None, n_new, hd), lambda b, h: (b, h, 0, 0)),
        out_shape=jax.ShapeDtypeStruct((nb, nh, n_new, hd), F32),
        compiler_params=_cparams(("arbitrary", "arbitrary")), name="attn_sample",
    )(q_s, k_new, v_new, cache_k, cache_v)


def _router_kernel(h_ref, g_ref, w_ref, b_ref, xn_ref, idx_ref, wt_ref):
    h = h_ref[...]
    xn = h * lax.rsqrt(jnp.mean(h * h, axis=-1, keepdims=True) + RMS_EPS) * g_ref[...]
    xn_ref[...] = xn.astype(xn_ref.dtype)
    x_hi, x_lo = _split2(xn)
    w = w_ref[...]
    w_hi, w_lo = _split2(w)
    logits = (jnp.dot(x_hi, w_hi, preferred_element_type=F32) + jnp.dot(x_lo, w_hi, preferred_element_type=F32)
              + jnp.dot(x_hi, w_lo, preferred_element_type=F32)) + b_ref[...]
    tm = logits.shape[0]
    lane = lax.broadcasted_iota(jnp.int32, logits.shape, 1)
    big = jnp.int32(V7X_LANES)
    is_g = lane < N_EXPERT_GROUPS
    gl = jnp.where(is_g, logits, NEG_INF)
    gmax = jnp.max(gl, axis=-1, keepdims=True)
    g_idx = jnp.min(jnp.where(jnp.logical_and(is_g, gl == gmax), lane, big), axis=-1, keepdims=True)
    g_w = 1.0 / jnp.sum(jnp.where(is_g, jnp.exp(gl - gmax), 0.0), axis=-1, keepdims=True)
    lo = N_EXPERT_GROUPS + g_idx * EXPERTS_PER_GROUP
    in_grp = jnp.logical_and(lane >= lo, lane < lo + EXPERTS_PER_GROUP)
    el = jnp.where(in_grp, logits, NEG_INF)
    v1 = jnp.max(el, axis=-1, keepdims=True)
    i1 = jnp.min(jnp.where(jnp.logical_and(in_grp, el == v1), lane, big), axis=-1, keepdims=True)
    el2 = jnp.where(lane == i1, NEG_INF, el)
    v2 = jnp.max(el2, axis=-1, keepdims=True)
    i2 = jnp.min(jnp.where(jnp.logical_and(in_grp, jnp.logical_and(el2 == v2, lane != i1)), lane, big),
                 axis=-1, keepdims=True)
    e2 = jnp.exp(v2 - v1)
    w1 = g_w / (1.0 + e2)
    w2 = g_w * e2 / (1.0 + e2)
    col = lax.broadcasted_iota(jnp.int32, (tm, 2), 1)
    idx_ref[...] = jnp.where(col == 0, i1, i2) - N_EXPERT_GROUPS
    wt_ref[...] = jnp.where(col == 0, w1, w2)


def router(h, gain, w_cat, b_cat):
    tp, d = h.shape
    tm = TOKEN_TILE
    return pl.pallas_call(
        _router_kernel, grid=(tp // tm,),
        in_specs=[pl.BlockSpec((tm, d), lambda i: (i, 0)),
                  pl.BlockSpec((1, d), lambda i: (0, 0)),
                  pl.BlockSpec((d, V7X_LANES), lambda i: (0, 0)),
                  pl.BlockSpec((1, V7X_LANES), lambda i: (0, 0))],
        out_specs=[pl.BlockSpec((tm, d), lambda i: (i, 0)),
                   pl.BlockSpec((tm, 2), lambda i: (i, 0)),
                   pl.BlockSpec((tm, 2), lambda i: (i, 0))],
        out_shape=[jax.ShapeDtypeStruct((tp, d), BF16),
                   jax.ShapeDtypeStruct((tp, 2), jnp.int32),
                   jax.ShapeDtypeStruct((tp, 2), F32)],
        compiler_params=_cparams(("parallel",)), name="router",
    )(h, gain, w_cat, b_cat)


RUN_ALIGN = 16
LOCAL_ROWS = 2 * TOKEN_TILE + N_EXPERTS * RUN_ALIGN
GRANULES = LOCAL_ROWS // RUN_ALIGN


def _for_pieces(n, chunk, fn):
    n_big = n // chunk
    lax.fori_loop(0, n_big, lambda j, c: (fn(j * chunk, chunk), c)[1], 0)
    done = n_big * chunk
    size = chunk // 2
    while size >= RUN_ALIGN:
        hit = (n & size) != 0
        pl.when(hit)(functools.partial(fn, done, size))
        done = done + jnp.where(hit, size, 0)
        size //= 2


def _for_granules(tot_ref, gran_ref, blk, fn):
    def one(j, carry):
        fn(pl.multiple_of(j * RUN_ALIGN, RUN_ALIGN), pl.multiple_of(gran_ref[blk * GRANULES + j], RUN_ALIGN))
        return carry
    lax.fori_loop(0, tot_ref[blk] // RUN_ALIGN, one, 0)


def _wait_rows(tot_ref, blk, wait_rows):
    _for_pieces(tot_ref[blk], LOCAL_ROWS // 2, lambda off, size: wait_rows(size))


def _dispatch_kernel(tot_ref, gran_ref, fill0_ref, filln_ref,
                     x_ref, src_ref, xs_hbm, loc, zeros, sem, zsem):
    b = pl.program_id(0)
    nb = pl.num_programs(0)
    slot = b % 2

    def copy(s, l, g, size):
        return pltpu.make_async_copy(loc.at[s, pl.ds(l, size)], xs_hbm.at[pl.ds(g, size)], sem.at[s])

    def wait_block(blk, s):
        _wait_rows(tot_ref, blk, lambda size: copy(s, 0, 0, size).wait())

    @pl.when(b >= 2)
    def _():
        wait_block(b - 2, slot)

    tok = lax.broadcasted_iota(jnp.int32, (LOCAL_ROWS, TOKEN_TILE), 1)
    pick = jnp.where(jnp.right_shift(src_ref[...], 1) == tok, 1.0, 0.0).astype(BF16)
    loc[slot] = jnp.dot(pick, x_ref[...], preferred_element_type=F32).astype(BF16)
    _for_granules(tot_ref, gran_ref, b, lambda l, g: copy(slot, l, g, RUN_ALIGN).start())

    @pl.when(b == nb - 1)
    def _():
        @pl.when(b >= 1)
        def _():
            wait_block(b - 1, 1 - slot)
        wait_block(b, slot)
        zeros[...] = jnp.zeros(zeros.shape, zeros.dtype)

        def fill(op):
            def per_expert(e, carry):
                f0 = fill0_ref[e]
                _for_pieces(filln_ref[e], MOE_ROW_TILE // 2, lambda off, size: op(pltpu.make_async_copy(
                    zeros.at[pl.ds(0, size)], xs_hbm.at[pl.ds(pl.multiple_of(f0 + off, RUN_ALIGN), size)], zsem)))
                return carry
            lax.fori_loop(0, fill0_ref.shape[0], per_expert, 0)

        fill(lambda c: c.start())
        fill(lambda c: c.wait())


def moe_dispatch(xn, plan):
    tp, d = xn.shape
    nb = tp // TOKEN_TILE
    grid_spec = pltpu.PrefetchScalarGridSpec(
        num_scalar_prefetch=4, grid=(nb,),
        in_specs=[pl.BlockSpec((TOKEN_TILE, d), lambda b, *_: (b, 0)),
                  pl.BlockSpec((None, LOCAL_ROWS, 1), lambda b, *_: (b, 0, 0))],
        out_specs=pl.BlockSpec(memory_space=pl.ANY),
        scratch_shapes=[pltpu.VMEM((2, LOCAL_ROWS, d), BF16), pltpu.VMEM((MOE_ROW_TILE // 2, d), BF16),
                        pltpu.SemaphoreType.DMA((2,)), pltpu.SemaphoreType.DMA],
    )
    return pl.pallas_call(
        _dispatch_kernel, grid_spec=grid_spec,
        out_shape=jax.ShapeDtypeStruct((plan["n_rows"], d), BF16),
        compiler_params=_cparams(("arbitrary",)), name="moe_dispatch",
    )(plan["tot"], plan["gran"], plan["fill0"], plan["filln"], xn, plan["src_col"])


def _expert_kernel(te_ref, nu_ref, first_ref, nxt_ref, par_ref,
                   xs_ref, wg_hbm, wu_hbm, wd_hbm, ys_ref,
                   wgf, wuf, wdf, wgb, wub, wdb, wsem, *, layer):
    i = pl.program_id(0)
    used = i < nu_ref[0]

    def weight_copies(e, s):
        return (pltpu.make_async_copy(wg_hbm.at[layer, e], wgf.at[s], wsem.at[s]),
                pltpu.make_async_copy(wu_hbm.at[layer, e], wuf.at[s], wsem.at[s]),
                pltpu.make_async_copy(wd_hbm.at[layer, e], wdf.at[s], wsem.at[s]))

    @pl.when(used)
    def _():
        @pl.when(first_ref[i] == 1)
        def _():
            s = par_ref[i]

            @pl.when(i == 0)
            def _():
                for c in weight_copies(te_ref[0], s):
                    c.start()

            for c in weight_copies(te_ref[i], s):
                c.wait()

            @pl.when(nxt_ref[i] >= 0)
            def _():
                for c in weight_copies(nxt_ref[i], 1 - s):
                    c.start()

            wgb[...] = wgf[s].astype(BF16)
            wub[...] = wuf[s].astype(BF16)
            wdb[...] = wdf[s].astype(BF16)

        x = xs_ref[...]
        hg = jnp.dot(x, wgb[...], preferred_element_type=F32)
        hu = jnp.dot(x, wub[...], preferred_element_type=F32)
        act = (_silu(hg) * hu).astype(BF16)
        ys_ref[...] = jnp.dot(act, wdb[...], preferred_element_type=F32).astype(ys_ref.dtype)

    @pl.when(jnp.logical_not(used))
    def _():
        ys_ref[...] = jnp.zeros(ys_ref.shape, ys_ref.dtype)


def expert_ffn(xs, plan, w_gate, w_up, w_down, layer):
    n_rows, d = xs.shape
    tm = MOE_ROW_TILE
    de = w_gate.shape[-1]

    grid_spec = pltpu.PrefetchScalarGridSpec(
        num_scalar_prefetch=5, grid=(n_rows // tm,),
        in_specs=[pl.BlockSpec((tm, d), lambda i, te, nu, *_: (jnp.minimum(i, nu[0] - 1), 0)),
                  pl.BlockSpec(memory_space=pl.ANY), pl.BlockSpec(memory_space=pl.ANY),
                  pl.BlockSpec(memory_space=pl.ANY)],
        out_specs=pl.BlockSpec((tm, d), lambda i, *_: (i, 0)),
        scratch_shapes=[pltpu.VMEM((2, d, de), F32), pltpu.VMEM((2, d, de), F32), pltpu.VMEM((2, de, d), F32),
                        pltpu.VMEM((d, de), BF16), pltpu.VMEM((d, de), BF16), pltpu.VMEM((de, d), BF16),
                        pltpu.SemaphoreType.DMA((2,))],
    )
    return pl.pallas_call(
        functools.partial(_expert_kernel, layer=layer), grid_spec=grid_spec,
        out_shape=jax.ShapeDtypeStruct((n_rows, d), BF16),
        compiler_params=_cparams(("arbitrary",)), name="expert_ffn",
    )(plan["tile_expert"], plan["n_used"], plan["tile_first"], plan["tile_next"], plan["tile_slot"],
      xs, w_gate, w_up, w_down)


def _combine_kernel(tot_ref, gran_ref,
                    ys_hbm, src_ref, wt_ref, h_ref, g_ref, *rest, n_out, emit_sum, head_blocks):
    outs, (buf, sem) = rest[:-2], rest[-2:]
    b = pl.program_id(0)
    nb = pl.num_programs(0)
    slot = b % 2

    def copy(s, l, g, size):
        return pltpu.make_async_copy(ys_hbm.at[pl.ds(g, size)], buf.at[s, pl.ds(l, size)], sem.at[s])

    def fetch(blk, s):
        _for_granules(tot_ref, gran_ref, blk, lambda l, g: copy(s, l, g, RUN_ALIGN).start())

    @pl.when(b == 0)
    def _():
        buf[...] = jnp.zeros(buf.shape, buf.dtype)
        fetch(0, 0)

    @pl.when(b + 1 < nb)
    def _():
        fetch(b + 1, 1 - slot)

    _wait_rows(tot_ref, b, lambda size: copy(slot, 0, 0, size).wait())
    y = buf[slot]
    tok2 = 2 * lax.broadcasted_iota(jnp.int32, (TOKEN_TILE, LOCAL_ROWS), 0)
    src = src_ref[...]
    h = h_ref[...]
    for k in range(2):
        pick = jnp.where(src == tok2 + k, 1.0, 0.0).astype(BF16)
        h = h + wt_ref[:, k:k + 1] * jnp.dot(pick, y, preferred_element_type=F32)
    if emit_sum:
        outs[0][...] = h
        outs = outs[1:]
    xn = h * lax.rsqrt(jnp.mean(h * h, axis=-1, keepdims=True) + RMS_EPS)
    if head_blocks is not None:
        val = xn * g_ref[0:1, :]

        @pl.when(b < head_blocks)
        def _():
            outs[0][...] = val

        @pl.when(b >= head_blocks)
        def _():
            outs[1][...] = val
        return
    for k in range(n_out):
        outs[k][...] = (xn * g_ref[k:k + 1, :]).astype(outs[k].dtype)


def moe_combine_norm(ys, plan, wts, h, gains, out_dtype, emit_sum, head_blocks=None):
    tp, d = h.shape
    nb = tp // TOKEN_TILE
    n_out = gains.shape[0]
    tok_spec = pl.BlockSpec((TOKEN_TILE, d), lambda b, *_: (b, 0))
    n_res = n_out + (1 if emit_sum else 0)
    if head_blocks is not None:
        assert n_out == 1 and not emit_sum and head_blocks == nb - 1
        out_specs = [pl.BlockSpec((TOKEN_TILE, d), lambda b, *_: (jnp.minimum(b, head_blocks - 1), 0)),
                     pl.BlockSpec((TOKEN_TILE, d), lambda b, *_: (0, 0))]
        out_shape = [jax.ShapeDtypeStruct((head_blocks * TOKEN_TILE, d), out_dtype),
                     jax.ShapeDtypeStruct((TOKEN_TILE, d), out_dtype)]
    else:
        out_specs = [tok_spec] * n_res
        out_shape = ([jax.ShapeDtypeStruct((tp, d), F32)] if emit_sum else []) + [
            jax.ShapeDtypeStruct((tp, d), out_dtype)] * n_out
    grid_spec = pltpu.PrefetchScalarGridSpec(
        num_scalar_prefetch=2, grid=(nb,),
        in_specs=[pl.BlockSpec(memory_space=pl.ANY),
                  pl.BlockSpec((None, 1, LOCAL_ROWS), lambda b, *_: (b, 0, 0)),
                  pl.BlockSpec((TOKEN_TILE, 2), lambda b, *_: (b, 0)),
                  tok_spec,
                  pl.BlockSpec((n_out, d), lambda b, *_: (0, 0))],
        out_specs=out_specs,
        scratch_shapes=[pltpu.VMEM((2, LOCAL_ROWS, d), BF16), pltpu.SemaphoreType.DMA((2,))],
    )
    return pl.pallas_call(
        functools.partial(_combine_kernel, n_out=n_out, emit_sum=emit_sum, head_blocks=head_blocks),
        grid_spec=grid_spec,
        out_shape=out_shape, compiler_params=_cparams(("arbitrary",)), name="moe_combine_norm",
    )(plan["tot"], plan["gran"], ys, plan["src_row"], wts, h, gains)


def moe_plan(idx, tp):
    i32 = jnp.int32
    tm = MOE_ROW_TILE
    nb = tp // TOKEN_TILE
    n_asg = 2 * TOKEN_TILE
    n_tiles = -(-(2 * tp + nb * N_EXPERTS * (RUN_ALIGN - 1) + N_EXPERTS * (tm - 1)) // tm)
    e_blk = idx.reshape(nb, n_asg)
    onehot = (e_blk[:, :, None] == jnp.arange(N_EXPERTS, dtype=i32)).astype(i32)
    tril = (jnp.arange(n_asg)[:, None] >= jnp.arange(n_asg)[None, :]).astype(F32)
    csum = jnp.einsum("ij,bje->bie", tril, onehot.astype(F32)).astype(i32)
    rank = jnp.sum(onehot * csum, axis=2) - 1
    plen = (csum[:, -1, :] + RUN_ALIGN - 1) // RUN_ALIGN * RUN_ALIGN
    loff = jnp.cumsum(plen, axis=1) - plen
    lpos = jnp.sum(onehot * loff[:, None, :], axis=2) + rank
    hit = lpos[:, :, None] == jnp.arange(LOCAL_ROWS, dtype=i32)
    src = jnp.sum(jnp.where(hit, jnp.arange(1, n_asg + 1, dtype=i32)[None, :, None], 0), axis=1) - 1
    seg = jnp.sum(plen, axis=0)
    seg_pad = (seg + tm - 1) // tm * tm
    gend = jnp.cumsum(seg_pad)
    gstart = gend - seg_pad
    gdst = gstart[None, :] + jnp.cumsum(plen, axis=0) - plen
    g_row = jnp.arange(GRANULES, dtype=i32) * RUN_ALIGN
    run_of = jnp.sum(((loff + plen)[:, None, :] <= g_row[None, :, None]).astype(i32), axis=2)
    run_hot = (run_of[:, :, None] == jnp.arange(N_EXPERTS, dtype=i32)).astype(i32)
    gran = jnp.sum(run_hot * (gdst - loff)[:, None, :], axis=2) + g_row[None, :]
    tot = jnp.sum(plen, axis=1)
    n_used = gend[-1] // tm
    t_row = jnp.minimum(jnp.arange(n_tiles, dtype=i32), n_used - 1) * tm
    tile_expert = jnp.minimum(jnp.sum(gend[None, :] <= t_row[:, None], axis=1), N_EXPERTS - 1)
    ids = jnp.arange(N_EXPERTS, dtype=i32)
    has = seg > 0
    later = jnp.logical_and(ids[None, :] > ids[:, None], has[None, :])
    nxt_e = jnp.min(jnp.where(later, ids[None, :], N_EXPERTS), axis=1)
    nxt_e = jnp.where(nxt_e < N_EXPERTS, nxt_e, -1)
    slot_e = (jnp.cumsum(has.astype(i32)) - 1) % 2
    te_hot = (tile_expert[:, None] == ids[None, :]).astype(i32)
    tile_first = jnp.concatenate([jnp.ones((1,), i32), (tile_expert[1:] != tile_expert[:-1]).astype(i32)])
    plan_tiles = {"tile_first": tile_first, "tile_next": jnp.sum(te_hot * nxt_e[None, :], axis=1),
                  "tile_slot": jnp.sum(te_hot * slot_e[None, :], axis=1)}
    return {
        **{k: v.astype(i32) for k, v in plan_tiles.items()},
        "tot": tot.astype(i32), "gran": gran.reshape(-1).astype(i32),
        "fill0": jnp.concatenate([gstart + seg, gend[-1:]]).astype(i32),
        "filln": jnp.concatenate([seg_pad - seg, n_tiles * tm - gend[-1:]]).astype(i32),
        "src_col": src[:, :, None].astype(i32), "src_row": src[:, None, :].astype(i32),
        "tile_expert": tile_expert.astype(i32), "n_used": n_used.reshape(1).astype(i32),
        "n_rows": n_tiles * tm,
    }


def kernel(x_prompt, x_sample, state_conv, state_ssm, cache_k, cache_v, a_norm, a_w_in, a_conv_w, a_conv_b, a_dt_bias, a_a_log, a_d_skip, a_gate_norm, a_w_out, kv_norm, w_kv, b_norm, b_w_q, b_w_o, moe_norm, router_group_w, router_group_b, router_expert_w, router_expert_b, expert_w_gate, expert_w_up, expert_w_down, final_norm):
    n_pb, seq, d_model = x_prompt.shape
    n_sb, n_new, _ = x_sample.shape
    n_a = a_w_in.shape[0]
    depth = moe_norm.shape[0]
    d_inner = a_w_out.shape[1]
    conv_dim = a_conv_w.shape[2]
    gn = (conv_dim - d_inner) // 2
    n_heads = d_inner // SSM_HEAD_DIM
    d_att = w_kv.shape[1] // 2
    n_p, n_s = n_pb * seq, n_sb * n_new
    lcm = MM_ROW_TILE * TOKEN_TILE // math.gcd(MM_ROW_TILE, TOKEN_TILE)
    tp = -(-(n_p + n_s) // lcm) * lcm
    L = SSD_CHUNK
    assert n_heads <= V7X_LANES and n_new <= L and n_new >= CONV_W - 1 and seq % (max(DILATIONS) * ATTN_Q_BLOCK) == 0

    def pad_rows(parts):
        rows = sum(p.shape[0] for p in parts)
        return jnp.concatenate(parts + [jnp.zeros((tp - rows,) + parts[0].shape[1:], parts[0].dtype)], axis=0)

    def lanes(v):
        return jnp.pad(v.astype(F32), ((0, 0), (0, V7X_LANES - n_heads)))[:, None, :]

    head_of_col = jnp.arange(d_inner, dtype=jnp.int32) // SSM_HEAD_DIM
    ssd_prm = {
        "conv_w": a_conv_w, "conv_b": a_conv_b[:, None, :],
        "dt_bias": lanes(a_dt_bias), "a_log": lanes(a_a_log),
        "d_skip": jnp.repeat(a_d_skip.astype(F32), SSM_HEAD_DIM, axis=1)[:, None, :],
        "gate_norm": a_gate_norm[:, None, :],
        "expand": (jnp.arange(V7X_LANES, dtype=jnp.int32)[:, None] == head_of_col[None, :]).astype(BF16),
    }
    w_dt = jnp.pad(a_w_in[:, :, 2 * d_inner + 2 * gn:], ((0, 0), (0, 0), (0, V7X_LANES - n_heads)))
    w_in_nk = jnp.swapaxes(a_w_in, 1, 2)
    r_w = jnp.concatenate([router_group_w, router_expert_w.reshape(depth, d_model, N_EXPERTS)], axis=2)
    r_w = jnp.pad(r_w, ((0, 0), (0, 0), (0, V7X_LANES - r_w.shape[2])))
    r_b = jnp.concatenate([router_group_b, router_expert_b.reshape(depth, N_EXPERTS)], axis=1)
    r_b = jnp.pad(r_b, ((0, 0), (0, V7X_LANES - r_b.shape[1])))[:, None, :]

    h = pad_rows([x_prompt.reshape(n_p, d_model), x_sample.reshape(n_s, d_model)])
    (xn,) = add_norm(h, [], a_norm[0:1], BF16, emit_sum=False)
    conv_p, ssm_p, conv_s, ssm_s = [], [], [], []
    kv = None
    for layer in range(depth):
        if layer < n_a:
            zxbc = matmul(xn, w_in_nk, layer, 2 * d_inner + 2 * gn, 1024, w_is_nk=True, tm=MM_ROW_TILE_WIDE)
            dt_raw = matmul(xn, w_dt, layer, V7X_LANES, V7X_LANES)
            y_p, cp, sp = ssd_mixer(zxbc, dt_raw, 0, n_pb, seq // L, L, None, None, ssd_prm, layer, d_inner, gn)
            precise = layer == 0
            if precise:
                x_s = x_sample.reshape(n_s, d_model)
                (xn_s,) = add_norm(x_s, [], a_norm[0:1], F32, emit_sum=False)
                zx_s = matmul_f32(xn_s, w_in_nk, layer, 2 * d_inner + 2 * gn, 1024, w_is_nk=True)
                dt_s = matmul_f32(xn_s, w_dt, layer, V7X_LANES, V7X_LANES)
            else:
                zx_s, dt_s = zxbc[n_p:n_p + n_s], dt_raw[n_p:n_p + n_s]
            zs = jnp.pad(zx_s.reshape(n_sb, n_new, -1), ((0, 0), (0, L - n_new), (0, 0)))
            ds = jnp.pad(dt_s.reshape(n_sb, n_new, -1), ((0, 0), (0, L - n_new), (0, 0)))
            y_s, cs, ss = ssd_mixer(zs.reshape(n_sb * L, -1), ds.reshape(n_sb * L, -1), 0, n_sb, 1, n_new,
                                    state_conv[layer], state_ssm[layer].reshape(n_sb, n_heads * SSM_HEAD_DIM, D_STATE),
                                    ssd_prm, layer, d_inner, gn, precise=precise)
            conv_p.append(cp); ssm_p.append(sp); conv_s.append(cs); ssm_s.append(ss)
            y_s = y_s.reshape(n_sb, L, d_inner)[:, :n_new].reshape(n_s, d_inner)
            y = pad_rows([y_p, y_s.astype(BF16)])
            h = matmul(y, a_w_out, layer, d_model, 512, res=h)
            if precise:
                h_s = matmul_f32(y_s, a_w_out, layer, d_model, 512, res=x_s)
                h = lax.dynamic_update_slice(h, h_s, (n_p, 0))
        else:
            b = layer - n_a
            q = matmul(xn, b_w_q, b, b_w_q.shape[2], 1024, tm=MM_ROW_TILE_WIDE)
            o_p = attn_prompt(q, kv, n_pb, seq)
            qs = q[n_p:n_p + n_s].reshape(n_sb, n_new, len(DILATIONS), N_KV_HEADS, ATT_HEAD_DIM)
            qs = qs.transpose(0, 3, 2, 1, 4).reshape(n_sb, N_KV_HEADS, len(DILATIONS) * n_new, ATT_HEAD_DIM)
            qs = jnp.pad(qs, ((0, 0), (0, 0), (0, 16 - len(DILATIONS) * n_new), (0, 0)))
            kvs = kv[n_p:n_p + n_s].reshape(n_sb, n_new, 2, N_KV_HEADS, ATT_HEAD_DIM).transpose(2, 0, 3, 1, 4)
            kvs = jnp.pad(kvs, ((0, 0), (0, 0), (0, 0), (0, V7X_LANES - n_new), (0, 0)))
            o_s = attn_sample(qs, kvs[0], kvs[1], cache_k.reshape(n_sb, -1, ATT_HEAD_DIM),
                              cache_v.reshape(n_sb, -1, ATT_HEAD_DIM), n_new)
            o = pad_rows([o_p, o_s.transpose(0, 2, 1, 3).reshape(n_s, d_att)])
            h = matmul(o, b_w_o, b, d_model, 1024, res=h)
        xm, idx, wts = router(h, moe_norm[layer:layer + 1], r_w[layer], r_b[layer])
        plan = moe_plan(idx, tp)
        ys = expert_ffn(moe_dispatch(xm, plan), plan, expert_w_gate, expert_w_up, expert_w_down, layer)
        if layer + 1 < n_a:
            h, xn = moe_combine_norm(ys, plan, wts, h, a_norm[layer + 1:layer + 2], BF16, emit_sum=True)
        elif layer + 1 == n_a:
            h, xkv, xn = moe_combine_norm(ys, plan, wts, h, jnp.stack([kv_norm, b_norm[0]]), BF16, emit_sum=True)
            kv = matmul(xkv, w_kv, None, w_kv.shape[1], 1024)
        elif layer + 1 < depth:
            h, xn = moe_combine_norm(ys, plan, wts, h, b_norm[layer + 1 - n_a:layer + 2 - n_a], BF16,
                                     emit_sum=True)
        else:
            y_head, y_tail = moe_combine_norm(ys, plan, wts, h, final_norm[None, :], F32, emit_sum=False,
                                              head_blocks=n_p // TOKEN_TILE)

    y_prompt = y_head.reshape(n_pb, seq, d_model)
    y_sample = y_tail[:n_s].reshape(n_sb, n_new, d_model)
    keep = min(max(DILATIONS) * (N_DIL_KEYS - 1), seq)
    kv_p = kv[:n_p].reshape(n_pb, seq, 2, N_KV_HEADS, ATT_HEAD_DIM)[:, seq - keep:]
    kv_s = kv[n_p:n_p + n_s].reshape(n_sb, n_new, 2, N_KV_HEADS, ATT_HEAD_DIM)
    ssm_shape = (n_heads, SSM_HEAD_DIM, D_STATE)
    return (y_prompt, y_sample,
            jnp.stack(conv_p), jnp.stack(ssm_p).reshape((n_a, n_pb) + ssm_shape),
            kv_p[:, :, 0], kv_p[:, :, 1],
            jnp.stack(conv_s), jnp.stack(ssm_s).reshape((n_a, n_sb) + ssm_shape),
            kv_s[:, :, 0], kv_s[:, :, 1])
```
